```python
import jax
import jax.numpy as jnp
from jax import lax
import numpy as np

D_MODEL = 4096
BATCH = 4
SEQ = 2048
DEPTH = 2
DEC_BATCH = 8
DEC_SEQ = 1
PAST_LEN = 16384
PAGE_SIZE = 128

GDN_HEAD_DIM = 128
GDN_HEADS = D_MODEL // 256
GDN_WIDTH = GDN_HEADS * GDN_HEAD_DIM
GDN_CHUNK = 64
CONV_W = 4
SB_HEAD_DIM = 128
SB_HEADS = D_MODEL // 256
SB_WIDTH = SB_HEADS * SB_HEAD_DIM
SB_BLOCK = 128
SB_BIAS_INIT = -10.0
MLP_GROUP_DIM = 128
MLP_GROUPS = D_MODEL // 256
MLP_WIDTH = MLP_GROUPS * MLP_GROUP_DIM
MLP_CHUNK = 128
D_FF = 256 * ((8 * D_MODEL // 3 + 255) // 256)
N_BRANCH = 3
IN_SIZES = (3 * GDN_WIDTH, GDN_WIDTH, GDN_HEADS, GDN_HEADS,
            SB_WIDTH, SB_WIDTH, SB_WIDTH, MLP_WIDTH, MLP_WIDTH, N_BRANCH * D_MODEL)
IN_WIDTH = sum(IN_SIZES)
ALPHA = (2 * DEPTH) ** 0.25
DN_BETA = (8 * DEPTH) ** -0.25
LN_EPS = 1e-5
NORM_EPS = 1e-6

kernel_name = 'hybrid_gdn_stickbreak_chunkmlp_step'


def layer_norm(x, g, b):
    xf = x.astype(jnp.float32)
    mu = jnp.mean(xf, axis=-1, keepdims=True)
    var = jnp.mean(jnp.square(xf - mu), axis=-1, keepdims=True)
    return ((xf - mu) * lax.rsqrt(var + LN_EPS) * g + b).astype(x.dtype)


def rms_norm(x, g):
    xf = x.astype(jnp.float32)
    return xf * lax.rsqrt(jnp.mean(jnp.square(xf), axis=-1, keepdims=True) + NORM_EPS) * g


def l2_norm(x):
    xf = x.astype(jnp.float32)
    return xf * lax.rsqrt(jnp.sum(jnp.square(xf), axis=-1, keepdims=True) + NORM_EPS)


def swiglu(x, w_gate, w_up, w_down):
    return (jax.nn.silu(x @ w_gate) * (x @ w_up)) @ w_down


def gdn_chunked(q, k, v, g, beta, s0):
    b, t, h, dk = q.shape
    dv = v.shape[-1]
    c = GDN_CHUNK
    n = t // c

    def blk(a):
        return jnp.swapaxes(a.reshape((b, n, c, h) + a.shape[3:]), 2, 3)

    q, k, v, g, beta = blk(q), blk(k), blk(v), blk(g), blk(beta)
    gc = jnp.cumsum(g, axis=-1)
    incl = jnp.tril(jnp.ones((c, c), bool))
    strict = jnp.tril(jnp.ones((c, c), bool), -1)
    decay = jnp.exp(jnp.where(incl, gc[..., :, None] - gc[..., None, :], -jnp.inf))
    kb = k * beta[..., None]
    lmat = jnp.where(strict, jnp.einsum('bnhid,bnhjd->bnhij', kb, k) * decay, 0.0)
    rhs = jnp.concatenate([v * beta[..., None], kb * jnp.exp(gc)[..., None]], axis=-1)
    sol = lax.linalg.triangular_solve(lmat, rhs, left_side=True, lower=True, unit_diagonal=True)
    u, w = sol[..., :dv], sol[..., dv:]
    qk = jnp.einsum('bnhid,bnhjd->bnhij', q, k) * decay

    def step(s, inp):
        q_i, k_i, u_i, w_i, gc_i, qk_i = inp
        v_new = u_i - jnp.einsum('bhcd,bhde->bhce', w_i, s)
        o = (jnp.einsum('bhcd,bhde->bhce', q_i * jnp.exp(gc_i)[..., None], s)
             + jnp.einsum('bhij,bhje->bhie', qk_i, v_new))
        g_last = gc_i[..., -1]
        s = (s * jnp.exp(g_last)[..., None, None]
             + jnp.einsum('bhcd,bhce->bhde', k_i * jnp.exp(g_last[..., None] - gc_i)[..., None], v_new))
        return s, o

    xs = tuple(jnp.swapaxes(a, 0, 1) for a in (q, k, u, w, gc, qk))
    s_fin, o = lax.scan(step, s0, xs)
    return o.transpose(1, 0, 3, 2, 4).reshape(b, t, h, dv), s_fin


def gdn_recurrent(q, k, v, g, beta, s0):
    def step(s, inp):
        q_t, k_t, v_t, g_t, b_t = inp
        s = s * jnp.exp(g_t)[..., None, None]
        kv = jnp.einsum('bhd,bhde->bhe', k_t, s)
        delta = (v_t - kv) * b_t[..., None]
        s = s + k_t[..., :, None] * delta[..., None, :]
        return s, jnp.einsum('bhd,bhde->bhe', q_t, s)

    xs = tuple(jnp.swapaxes(a, 0, 1) for a in (q, k, v, g, beta))
    s_fin, o = lax.scan(step, s0, xs)
    return jnp.swapaxes(o, 0, 1), s_fin


def stick_breaking(q, k, v, q_pos, k_pos, bias):
    z = (jnp.einsum('bqhd,bkhd->bhqk', q, k).astype(jnp.float32) * SB_HEAD_DIM ** -0.5
         + bias.astype(jnp.float32)[None, :, None, None])
    past = k_pos[None, :] < q_pos[:, None]
    log_keep = jnp.where(past, jax.nn.log_sigmoid(-z), 0.0)
    log_after = lax.cumsum(log_keep, axis=3, reverse=True) - log_keep
    w = jnp.where(past, jnp.exp(jax.nn.log_sigmoid(z) + log_after), 0.0)
    return jnp.einsum('bhqk,bkhd->bqhd', w.astype(v.dtype), v)


def sb_prompt(q, k, v, bias):
    b, t, h, d = q.shape
    nb = t // SB_BLOCK
    qb = jnp.swapaxes(q.reshape(b, nb, SB_BLOCK, h, d), 0, 1)
    k_pos = jnp.arange(t)

    def one(args):
        q_blk, i = args
        return stick_breaking(q_blk, k, v, i * SB_BLOCK + jnp.arange(SB_BLOCK), k_pos, bias)

    o = lax.map(one, (qb, jnp.arange(nb)))
    return jnp.swapaxes(o, 0, 1).reshape(b, t, h, d)


def chunk_mlp(u, vn, mlp_ws, mlp_bs):
    b, t, _ = vn.shape
    lc = min(t, MLP_CHUNK)
    n = t // lc
    tri = jnp.tril(jnp.ones((MLP_CHUNK, MLP_CHUNK), bool))
    ws = jnp.where(tri, mlp_ws, 0.0)[:, :lc, :lc]
    vv = vn.reshape(b, n, lc, MLP_GROUPS, MLP_GROUP_DIM)
    mixed = jnp.einsum('gij,bnjgd->bnigd', ws, vv) + mlp_bs[:, :lc].T[None, None, :, :, None]
    return u * mixed.reshape(b, t, MLP_WIDTH).astype(u.dtype)


def token_mix(h, prompt, conv_ctx, s0, k_past, v_past, w_in, conv_w, a_log, dt_bias,
              gdn_norm_g, sb_bias, mlp_ln_g, mlp_ln_b, mlp_ws, mlp_bs, w_branch_a, w_branch_b,
              w_branch_c, w_out):
    b, t, _ = h.shape
    splits = np.cumsum(IN_SIZES)[:-1].tolist()
    (dn_qkv, dn_z, dn_b, dn_a, sb_q, sb_k, sb_v, mlp_u, mlp_v,
     gate_logits) = jnp.split(h @ w_in, splits, axis=-1)

    xc = jnp.concatenate([conv_ctx.astype(dn_qkv.dtype), dn_qkv], axis=1)
    conv_new = xc[:, t:]
    conv = xc[:, :t] * conv_w[0]
    for j in range(1, CONV_W):
        conv = conv + xc[:, j:j + t] * conv_w[j]
    dq, dk, dv = jnp.split(jax.nn.silu(conv), 3, axis=-1)

    def gh(a):
        return a.reshape(b, t, GDN_HEADS, GDN_HEAD_DIM)

    q = l2_norm(gh(dq)) * GDN_HEAD_DIM ** -0.5
    k = l2_norm(gh(dk))
    v = gh(dv).astype(jnp.float32)
    beta = jax.nn.sigmoid(dn_b.astype(jnp.float32))
    g = -jnp.exp(a_log.astype(jnp.float32)) * jax.nn.softplus(dn_a.astype(jnp.float32) + dt_bias)
    core = gdn_chunked if prompt else gdn_recurrent
    o, s_new = core(q, k, v, g, beta, s0.astype(jnp.float32))
    o = rms_norm(o, gdn_norm_g) * jax.nn.silu(gh(dn_z).astype(jnp.float32))
    branch_a = o.reshape(b, t, GDN_WIDTH).astype(h.dtype) @ w_branch_a

    def sh(a):
        return a.reshape(b, t, SB_HEADS, SB_HEAD_DIM)

    q_s, k_s, v_s = sh(sb_q), sh(sb_k), sh(sb_v)
    if prompt:
        o_s = sb_prompt(q_s, k_s, v_s, sb_bias)
    else:
        past = k_past.shape[1]
        k_all = jnp.concatenate([k_past.astype(k_s.dtype), k_s], axis=1)
        v_all = jnp.concatenate([v_past.astype(v_s.dtype), v_s], axis=1)
        o_s = stick_breaking(q_s, k_all, v_all, past + jnp.arange(t), jnp.arange(past + t), sb_bias)
    branch_b = o_s.reshape(b, t, SB_WIDTH) @ w_branch_b

    u = jax.nn.gelu(mlp_u)
    vn = layer_norm(jax.nn.gelu(mlp_v), mlp_ln_g, mlp_ln_b)
    branch_c = chunk_mlp(u, vn, mlp_ws, mlp_bs) @ w_branch_c

    g_a, g_b, g_c = jnp.split(jax.nn.sigmoid(gate_logits), N_BRANCH, axis=-1)
    y = (g_a * branch_a + g_b * branch_b + g_c * branch_c) @ w_out
    return y, (s_new, conv_new, k_s, v_s, vn)


def run_layer(x, prompt, conv_ctx, s0, k_past, v_past, norms, ffns, mix_w):
    (g1, b1), (g2, b2), (g3, b3) = norms
    x = layer_norm(ALPHA * x + 0.5 * swiglu(x, *ffns[0]), g1, b1)
    y, state = token_mix(x, prompt, conv_ctx, s0, k_past, v_past, *mix_w)
    x = layer_norm(ALPHA * x + y, g2, b2)
    x = layer_norm(ALPHA * x + 0.5 * swiglu(x, *ffns[1]), g3, b3)
    return x, state


def setup_inputs(seed: int = 0) -> dict:
    key = jax.random.key(seed)
    ks = iter(jax.random.split(key, 48))
    f32 = jnp.float32

    def nrm(shape, scale=1.0):
        return scale * jax.random.normal(next(ks), shape, f32)

    def gain(shape):
        return 1.0 + nrm(shape, 0.02)

    n_pages = PAST_LEN // PAGE_SIZE
    n_used = DEC_BATCH * n_pages
    n_pool = n_used + max(1, n_used // 4)
    page_table = jax.random.permutation(next(ks), n_pool)[:n_used].reshape(DEC_BATCH, n_pages).astype(jnp.int32)
    L = DEPTH
    dsc = D_MODEL ** -0.5
    return {
        'x_prompt': nrm((BATCH, SEQ, D_MODEL)),
        'x_sample': nrm((DEC_BATCH, DEC_SEQ, D_MODEL)),
        'state_gdn': nrm((L, DEC_BATCH, GDN_HEADS, GDN_HEAD_DIM, GDN_HEAD_DIM), 0.1),
        'state_conv': nrm((L, DEC_BATCH, CONV_W - 1, 3 * GDN_WIDTH)),
        'cache_k': nrm((L, n_pool, PAGE_SIZE, SB_HEADS, SB_HEAD_DIM)),
        'cache_v': nrm((L, n_pool, PAGE_SIZE, SB_HEADS, SB_HEAD_DIM)),
        'page_table': page_table,
        'ln1_g': gain((L, D_MODEL)),
        'ln1_b': nrm((L, D_MODEL), 0.02),
        'ffn1_wg': nrm((L, D_MODEL, D_FF), DN_BETA * dsc),
        'ffn1_wu': nrm((L, D_MODEL, D_FF), DN_BETA * dsc),
        'ffn1_wd': nrm((L, D_FF, D_MODEL), DN_BETA * D_FF ** -0.5),
        'w_in': nrm((L, D_MODEL, IN_WIDTH), dsc),
        'conv_w': nrm((L, CONV_W, 3 * GDN_WIDTH), CONV_W ** -0.5),
        'a_log': jnp.log(jax.random.uniform(next(ks), (L, GDN_HEADS), f32, 1.0, 16.0)),
        'dt_bias': gain((L, GDN_HEADS)),
        'gdn_norm_g': gain((L, GDN_HEAD_DIM)),
        'sb_bias': SB_BIAS_INIT + nrm((L, SB_HEADS), 0.1),
        'mlp_ln_g': gain((L, MLP_WIDTH)),
        'mlp_ln_b': nrm((L, MLP_WIDTH), 0.02),
        'mlp_ws': nrm((L, MLP_GROUPS, MLP_CHUNK, MLP_CHUNK), MLP_CHUNK ** -0.5),
        'mlp_bs': gain((L, MLP_GROUPS, MLP_CHUNK)),
        'w_branch_a': nrm((L, GDN_WIDTH, D_MODEL), DN_BETA * GDN_WIDTH ** -0.5),
        'w_branch_b': nrm((L, SB_WIDTH, D_MODEL), DN_BETA * SB_WIDTH ** -0.5),
        'w_branch_c': nrm((L, MLP_WIDTH, D_MODEL), DN_BETA * MLP_WIDTH ** -0.5),
        'w_out': nrm((L, D_MODEL, D_MODEL), DN_BETA * dsc),
        'ln2_g': gain((L, D_MODEL)),
        'ln2_b': nrm((L, D_MODEL), 0.02),
        'ffn2_wg': nrm((L, D_MODEL, D_FF), DN_BETA * dsc),
        'ffn2_wu': nrm((L, D_MODEL, D_FF), DN_BETA * dsc),
        'ffn2_wd': nrm((L, D_FF, D_MODEL), DN_BETA * D_FF ** -0.5),
        'ln3_g': gain((L, D_MODEL)),
        'ln3_b': nrm((L, D_MODEL), 0.02),
    }


def reference(x_prompt, x_sample, state_gdn, state_conv, cache_k, cache_v, page_table,
              ln1_g, ln1_b, ffn1_wg, ffn1_wu, ffn1_wd, w_in, conv_w, a_log, dt_bias,
              gdn_norm_g, sb_bias, mlp_ln_g, mlp_ln_b, mlp_ws, mlp_bs, w_branch_a, w_branch_b,
              w_branch_c, w_out, ln2_g, ln2_b, ffn2_wg, ffn2_wu, ffn2_wd, ln3_g, ln3_b):
    n_seq, n_pages = page_table.shape
    bp = x_prompt.shape[0]
    hp, hs = x_prompt, x_sample
    st_p, st_s = [], []
    for l in range(DEPTH):
        norms = ((ln1_g[l], ln1_b[l]), (ln2_g[l], ln2_b[l]), (ln3_g[l], ln3_b[l]))
        ffns = ((ffn1_wg[l], ffn1_wu[l], ffn1_wd[l]), (ffn2_wg[l], ffn2_wu[l], ffn2_wd[l]))
        mix_w = (w_in[l], conv_w[l], a_log[l], dt_bias[l], gdn_norm_g[l], sb_bias[l],
                 mlp_ln_g[l], mlp_ln_b[l], mlp_ws[l], mlp_bs[l], w_branch_a[l], w_branch_b[l],
                 w_branch_c[l], w_out[l])
        conv0 = jnp.zeros((bp, CONV_W - 1, 3 * GDN_WIDTH), x_prompt.dtype)
        s0 = jnp.zeros((bp, GDN_HEADS, GDN_HEAD_DIM, GDN_HEAD_DIM), jnp.float32)
        hp, sp = run_layer(hp, True, conv0, s0, None, None, norms, ffns, mix_w)
        k_past = cache_k[l][page_table].reshape(n_seq, n_pages * PAGE_SIZE, SB_HEADS, SB_HEAD_DIM)
        v_past = cache_v[l][page_table].reshape(n_seq, n_pages * PAGE_SIZE, SB_HEADS, SB_HEAD_DIM)
        hs, ss = run_layer(hs, False, state_conv[l], state_gdn[l], k_past, v_past, norms, ffns, mix_w)
        st_p.append(sp)
        st_s.append(ss)
    gdn_prompt = jnp.stack([s[0] for s in st_p])
    conv_prompt = jnp.stack([s[1] for s in st_p])
    k_prompt = jnp.stack([s[2] for s in st_p])
    v_prompt = jnp.stack([s[3] for s in st_p])
    gdn_sample = jnp.stack([s[0] for s in st_s])
    conv_sample = jnp.stack([s[1] for s in st_s])
    k_sample = jnp.stack([s[2] for s in st_s])
    v_sample = jnp.stack([s[3] for s in st_s])
    mlpv_sample = jnp.stack([s[4] for s in st_s])
    return (hp, hs, gdn_prompt, conv_prompt, k_prompt, v_prompt,
            gdn_sample, conv_sample, k_sample, v_sample, mlpv_sample)
```

```python
import functools

import jax
import jax.numpy as jnp
from jax import lax
from jax.experimental import pallas as pl
from jax.experimental.pallas import tpu as pltpu

F32 = jnp.float32
BF16 = jnp.bfloat16

LANES = 128
SUBLANES = 8
BF16_ROWS = 16
VMEM_LIMIT = 56 * 1024 * 1024
LN_EPS = 1e-5
NORM_EPS = 1e-6
GDN_CHUNK = 64
SB_BLOCK = 128
MLP_CHUNK = 128
GDN_HEADS_PER_STEP = 4
NT = (((1,), (1,)), ((), ()))
TN = (((0,), (0,)), ((), ()))


def _params(*sem):
    return pltpu.CompilerParams(dimension_semantics=sem, vmem_limit_bytes=VMEM_LIMIT)


def _tile(n, prefs):
    for t in prefs:
        if n % t == 0:
            return t
    return n


def _dot(a, b):
    return jnp.dot(a, b, preferred_element_type=F32)


def _sigmoid(x):
    return 1.0 / (1.0 + jnp.exp(-x))


def _softplus(x):
    return jnp.maximum(x, 0.0) + jnp.log1p(jnp.exp(-jnp.abs(x)))


def _gelu(x):
    return 0.5 * x * (1.0 + jnp.tanh(0.7978845608028654 * (x + 0.044715 * (x * x * x))))


def _split(a):
    hi = a.astype(BF16)
    lo = (a - hi.astype(F32)).astype(BF16)
    return hi, lo


def _dot_split(a, b):
    return _dot(a[0], b[0]) + _dot(a[0], b[1]) + _dot(a[1], b[0])


def _split3(a):
    hi = a.astype(BF16)
    r = a - hi.astype(F32)
    mid = r.astype(BF16)
    lo = (r - mid.astype(F32)).astype(BF16)
    return hi, mid, lo


def _iota2(shape, dim):
    return lax.broadcasted_iota(jnp.int32, shape, dim)


def _mm_kernel(x_ref, w_ref, *refs, epilogue, n_heads):
    acc = _dot(x_ref[...], w_ref[...])
    outs = refs
    if epilogue == "sigmoid":
        acc = _sigmoid(acc)
    elif epilogue == "gdn_gate":
        a_ref, dt_ref = refs[:2]
        outs = refs[2:]
        lane = _iota2(acc.shape, 1)
        g = -jnp.exp(a_ref[...]) * _softplus(acc + dt_ref[...])
        acc = jnp.where(lane < n_heads, _sigmoid(acc), jnp.where(lane < 2 * n_heads, g, jnp.exp(g)))
    for o in outs:
        o[...] = acc.astype(o.dtype)


def _mm(x, w, out_dtypes=(F32,), epilogue=None, extras=(), n_heads=0, name="mm"):
    m, k = x.shape
    n = w.shape[1]
    tm = _tile(m, (1024, 512, 256))
    tn = _tile(n, (512, 256, 128))
    outs = pl.pallas_call(
        functools.partial(_mm_kernel, epilogue=epilogue, n_heads=n_heads),
        grid=(m // tm, n // tn),
        in_specs=[pl.BlockSpec((tm, k), lambda i, j: (i, 0)),
                  pl.BlockSpec((k, tn), lambda i, j: (0, j))]
                 + [pl.BlockSpec((1, tn), lambda i, j: (0, j)) for _ in extras],
        out_specs=[pl.BlockSpec((tm, tn), lambda i, j: (i, j)) for _ in out_dtypes],
        out_shape=[jax.ShapeDtypeStruct((m, n), d) for d in out_dtypes],
        compiler_params=_params("parallel", "arbitrary"),
        name=name,
    )(x, w, *extras)
    return outs if len(out_dtypes) > 1 else outs[0]


def _ffn_up_kernel(x_ref, wg_ref, wu_ref, o_ref):
    x = x_ref[...]
    g = _dot(x, wg_ref[...])
    u = _dot(x, wu_ref[...])
    o_ref[...] = (g * _sigmoid(g) * u).astype(o_ref.dtype)


def _ffn_up(x, wg, wu):
    m, k = x.shape
    n = wg.shape[1]
    tm = _tile(m, (1024, 512, 256))
    tn = _tile(n, (256, 128))
    return pl.pallas_call(
        _ffn_up_kernel,
        grid=(m // tm, n // tn),
        in_specs=[pl.BlockSpec((tm, k), lambda i, j: (i, 0)),
                  pl.BlockSpec((k, tn), lambda i, j: (0, j)),
                  pl.BlockSpec((k, tn), lambda i, j: (0, j))],
        out_specs=pl.BlockSpec((tm, tn), lambda i, j: (i, j)),
        out_shape=jax.ShapeDtypeStruct((m, n), BF16),
        compiler_params=_params("parallel", "arbitrary"),
        name="ffn_up",
    )(x, wg, wu)


def _mm_res_kernel(h_ref, w_ref, x_ref, o_ref, *, alpha, scale):
    o_ref[...] = alpha * x_ref[...] + scale * _dot(h_ref[...], w_ref[...])


def _mm_res(h, w, x, alpha, scale, name):
    m, k = h.shape
    n = w.shape[1]
    tm = _tile(m, (512, 256))
    tn = _tile(n, (256, 128))
    return pl.pallas_call(
        functools.partial(_mm_res_kernel, alpha=alpha, scale=scale),
        grid=(m // tm, n // tn),
        in_specs=[pl.BlockSpec((tm, k), lambda i, j: (i, 0)),
                  pl.BlockSpec((k, tn), lambda i, j: (0, j)),
                  pl.BlockSpec((tm, tn), lambda i, j: (i, j))],
        out_specs=pl.BlockSpec((tm, tn), lambda i, j: (i, j)),
        out_shape=jax.ShapeDtypeStruct((m, n), F32),
        compiler_params=_params("parallel", "arbitrary"),
        name=name,
    )(h, w, x)


def _ln_kernel(r_ref, g_ref, b_ref, o_ref, ob_ref, *, rows):
    g = g_ref[...]
    b = b_ref[...]

    def body(i, carry):
        sl = pl.ds(pl.multiple_of(i * rows, rows), rows)
        r = r_ref[sl, :]
        mu = jnp.mean(r, axis=-1, keepdims=True)
        c = r - mu
        var = jnp.mean(c * c, axis=-1, keepdims=True)
        y = c * lax.rsqrt(var + LN_EPS) * g + b
        o_ref[sl, :] = y
        ob_ref[sl, :] = y.astype(BF16)
        return carry

    lax.fori_loop(0, r_ref.shape[0] // rows, body, 0)


def _ln(r, g, b):
    m, d = r.shape
    tr = _tile(m, (256,))
    rows = _tile(tr, (32, BF16_ROWS))
    return pl.pallas_call(
        functools.partial(_ln_kernel, rows=rows),
        grid=(m // tr,),
        in_specs=[pl.BlockSpec((tr, d), lambda i: (i, 0)),
                  pl.BlockSpec((1, d), lambda i: (0, 0)),
                  pl.BlockSpec((1, d), lambda i: (0, 0))],
        out_specs=[pl.BlockSpec((tr, d), lambda i: (i, 0)),
                   pl.BlockSpec((tr, d), lambda i: (i, 0))],
        out_shape=[jax.ShapeDtypeStruct((m, d), F32), jax.ShapeDtypeStruct((m, d), BF16)],
        compiler_params=_params("parallel"),
        name="layer_norm",
    )(r, g.reshape(1, d), b.reshape(1, d))


def _merge_kernel(oa_ref, ob_ref, oc_ref, wa_ref, wb_ref, wc_ref, ga_ref, gb_ref, gc_ref, o_ref):
    y = ga_ref[...] * _dot(oa_ref[...], wa_ref[...])
    y = y + gb_ref[...] * _dot(ob_ref[...], wb_ref[...])
    y = y + gc_ref[...] * _dot(oc_ref[...], wc_ref[...])
    o_ref[...] = y.astype(o_ref.dtype)


def _merge(oa, ob, oc, wa, wb, wc, gates):
    m, k = oa.shape
    d = wa.shape[1]
    tm = _tile(m, (512, 256))
    tn = _tile(d, (512, 256, 128))
    nb = d // tn
    o_spec = pl.BlockSpec((tm, k), lambda i, j: (i, 0))
    w_spec = pl.BlockSpec((k, tn), lambda i, j: (0, j))
    return pl.pallas_call(
        _merge_kernel,
        grid=(m // tm, nb),
        in_specs=[o_spec, o_spec, o_spec, w_spec, w_spec, w_spec,
                  pl.BlockSpec((tm, tn), lambda i, j: (i, j)),
                  pl.BlockSpec((tm, tn), lambda i, j: (i, nb + j)),
                  pl.BlockSpec((tm, tn), lambda i, j: (i, 2 * nb + j))],
        out_specs=pl.BlockSpec((tm, tn), lambda i, j: (i, j)),
        out_shape=jax.ShapeDtypeStruct((m, d), BF16),
        compiler_params=_params("parallel", "arbitrary"),
        name="branch_merge",
    )(oa, ob, oc, wa, wb, wc, gates, gates, gates)


def _ffn(x, xb, wg, wu, wd, g, b, alpha):
    h = _ffn_up(xb, wg, wu)
    r = _mm_res(h, wd, x, alpha, 0.5, "ffn_down")
    return _ln(r, g, b)


def _causal_conv_silu(x, prev, cw):
    acc = x * cw[3:4, :]
    row = _iota2(prev.shape, 0)
    for s in (1, 2, 3):
        xs = pltpu.roll(x, s, 0)
        ps = pltpu.roll(prev, s, 0)
        head = jnp.where(row < s, ps, xs[:SUBLANES])
        xs = jnp.concatenate([head, xs[SUBLANES:]], axis=0)
        acc = acc + xs * cw[3 - s:4 - s, :]
    return acc * _sigmoid(acc)


def _l2norm(x):
    return x * lax.rsqrt(jnp.sum(x * x, axis=-1, keepdims=True) + NORM_EPS)


def _gdn_prep_kernel(xq_ref, xk_ref, xv_ref, pq_ref, pk_ref, pv_ref, cwq_ref, cwk_ref, cwv_ref,
                     beta_ref, g_ref, gt_ref,
                     u_ref, w_ref, qg_ref, kd_ref, qk_ref, eg_ref, *, hb):
    c = GDN_CHUNK
    has_prev = pl.program_id(1) > 0

    def conv(x_ref, p_ref, cw_ref):
        prev = jnp.where(has_prev, p_ref[0], 0.0)
        return _causal_conv_silu(x_ref[0], prev, cw_ref[...])

    qs = conv(xq_ref, pq_ref, cwq_ref)
    ks = conv(xk_ref, pk_ref, cwk_ref)
    vs = conv(xv_ref, pv_ref, cwv_ref)

    r = _iota2((c, c), 0)
    cc = _iota2((c, c), 1)
    incl = r >= cc
    strict = r > cc
    eye = jnp.where(r == cc, 1.0, 0.0)
    tril = jnp.where(incl, 1.0, 0.0).astype(BF16)
    triu = jnp.where(r <= cc, 1.0, 0.0).astype(BF16)
    g3 = _split3(g_ref[0, 0])
    gc_col = _dot(tril, g3[0]) + _dot(tril, g3[1]) + _dot(tril, g3[2])
    t3 = _split3(gt_ref[0, 0, 0])
    gc_row = _dot(t3[0], triu) + _dot(t3[1], triu) + _dot(t3[2], triu)
    beta_all = beta_ref[0, 0]

    for hh in range(hb):
        sl = slice(hh * LANES, (hh + 1) * LANES)
        q = _l2norm(qs[:, sl]) * (LANES ** -0.5)
        k = _l2norm(ks[:, sl])
        v = vs[:, sl]
        beta = beta_all[:, hh:hh + 1]
        gcc = gc_col[:, hh:hh + 1]
        gcr = gc_row[hh:hh + 1, :]
        decay = jnp.where(incl, jnp.exp(jnp.minimum(gcc - gcr, 0.0)), 0.0)
        kb = k * beta
        kbf = k.astype(BF16)
        kk = lax.dot_general(kb.astype(BF16), kbf, NT, preferred_element_type=F32)
        lmat = jnp.where(strict, kk * decay, 0.0)
        egc = jnp.exp(gcc)
        rhs = jnp.concatenate([v * beta, kb * egc], axis=1)
        tinv = eye - lmat
        p = _split(lmat)
        for _ in range(5):
            pw = _dot_split(p, p)
            p = _split(pw)
            tinv = tinv + _dot_split(_split(tinv), p)
        sol = _dot_split(_split(tinv), _split(rhs))
        qk = lax.dot_general(q.astype(BF16), kbf, NT, preferred_element_type=F32) * decay
        glast = gcc[c - 1:c, :]
        u_ref[0, 0, hh] = sol[:, :LANES]
        w_ref[0, 0, hh] = sol[:, LANES:].astype(BF16)
        qg_ref[0, 0, hh] = (q * egc).astype(BF16)
        kd_ref[0, 0, hh] = (k * jnp.exp(glast - gcc)).astype(BF16)
        qk_ref[0, 0, hh] = qk.astype(BF16)
        eg_ref[0, 0, hh] = jnp.broadcast_to(jnp.exp(glast), (1, LANES))


def _gdn_scan_kernel(u_ref, w_ref, qg_ref, kd_ref, qk_ref, eg_ref, z_ref, gn_ref, o_ref, s_ref, *, heads):
    @pl.when(pl.program_id(1) == 0)
    def _():
        s_ref[...] = jnp.zeros_like(s_ref)

    gn = gn_ref[...]
    for h in range(heads):
        s = s_ref[0, h]
        sb = s.astype(BF16)
        v_new = u_ref[0, 0, h] - _dot(w_ref[0, 0, h], sb)
        vb = v_new.astype(BF16)
        o = _dot(qg_ref[0, 0, h], sb) + _dot(qk_ref[0, 0, h], vb)
        s_ref[0, h] = s * eg_ref[0, 0, h] + lax.dot_general(kd_ref[0, 0, h], vb, TN,
                                                           preferred_element_type=F32)
        o = o * lax.rsqrt(jnp.mean(o * o, axis=-1, keepdims=True) + NORM_EPS) * gn
        z = z_ref[0, :, h * LANES:(h + 1) * LANES]
        o_ref[0, :, h * LANES:(h + 1) * LANES] = (o * (z * _sigmoid(z))).astype(o_ref.dtype)


def _gdn_prompt(zg, bg, conv_w, gdn_norm_g, heads):
    b, t, _ = zg.shape
    c = GDN_CHUNK
    n = t // c
    hb = GDN_HEADS_PER_STEP
    hg = heads // hb
    gw = heads * LANES
    wb = hb * LANES
    nq = gw // wb

    def regroup(a):
        a = a.reshape(b, t, hg, hb).transpose(0, 2, 1, 3)
        return jnp.pad(a, ((0, 0), (0, 0), (0, 0), (0, LANES - hb)))

    beta = regroup(bg[:, :, :heads])
    g = bg[:, :, heads:2 * heads]
    gcol = regroup(g)
    gt = g.reshape(b, n, c, hg, hb).transpose(0, 3, 1, 4, 2)
    gt = jnp.pad(gt, ((0, 0), (0, 0), (0, 0), (0, SUBLANES - hb), (0, 0)))

    def xspec(off):
        return pl.BlockSpec((1, c, wb), lambda i, j, h: (i, j, off + h))

    def pspec(off):
        return pl.BlockSpec((1, SUBLANES, wb),
                            lambda i, j, h: (i, jnp.maximum(j * (c // SUBLANES) - 1, 0), off + h))

    def cspec(off):
        return pl.BlockSpec((4, wb), lambda i, j, h: (0, off + h))

    def ospec(rows, cols):
        return pl.BlockSpec((1, 1, hb, rows, cols), lambda i, j, h: (i, j, h, 0, 0))

    def oshape(rows, cols, dt):
        return jax.ShapeDtypeStruct((b, n, heads, rows, cols), dt)

    u, w, qg, kd, qk, eg = pl.pallas_call(
        functools.partial(_gdn_prep_kernel, hb=hb),
        grid=(b, n, hg),
        in_specs=[xspec(0), xspec(nq), xspec(2 * nq), pspec(0), pspec(nq), pspec(2 * nq),
                  cspec(0), cspec(nq), cspec(2 * nq),
                  pl.BlockSpec((1, 1, c, LANES), lambda i, j, h: (i, h, j, 0)),
                  pl.BlockSpec((1, 1, c, LANES), lambda i, j, h: (i, h, j, 0)),
                  pl.BlockSpec((1, 1, 1, SUBLANES, c), lambda i, j, h: (i, h, j, 0, 0))],
        out_specs=[ospec(c, LANES), ospec(c, LANES), ospec(c, LANES), ospec(c, LANES),
                   ospec(c, c), ospec(1, LANES)],
        out_shape=[oshape(c, LANES, F32), oshape(c, LANES, BF16), oshape(c, LANES, BF16),
                   oshape(c, LANES, BF16), oshape(c, c, BF16), oshape(1, LANES, F32)],
        compiler_params=_params("parallel", "parallel", "parallel"),
        name="gdn_prep",
    )(zg, zg, zg, zg, zg, zg, conv_w, conv_w, conv_w, beta, gcol, gt)

    def sspec(rows, cols):
        return pl.BlockSpec((1, 1, heads, rows, cols), lambda i, j: (i, j, 0, 0, 0))

    o, s_fin = pl.pallas_call(
        functools.partial(_gdn_scan_kernel, heads=heads),
        grid=(b, n),
        in_specs=[sspec(c, LANES), sspec(c, LANES), sspec(c, LANES), sspec(c, LANES),
                  sspec(c, c), sspec(1, LANES),
                  pl.BlockSpec((1, c, gw), lambda i, j: (i, j, 3)),
                  pl.BlockSpec((1, LANES), lambda i, j: (0, 0))],
        out_specs=[pl.BlockSpec((1, c, gw), lambda i, j: (i, j, 0)),
                   pl.BlockSpec((1, heads, LANES, LANES), lambda i, j: (i, 0, 0, 0))],
        out_shape=[jax.ShapeDtypeStruct((b, t, gw), BF16),
                   jax.ShapeDtypeStruct((b, heads, LANES, LANES), F32)],
        compiler_params=_params("parallel", "arbitrary"),
        name="gdn_scan",
    )(u, w, qg, kd, qk, eg, zg, gdn_norm_g.reshape(1, LANES))
    return o, s_fin


def _gdn_sample_prep_kernel(x_ref, ctx_ref, cw_ref, o_ref, *, heads):
    cw = cw_ref[...]
    acc = x_ref[...] * cw[3:4, :]
    for j in range(3):
        acc = acc + ctx_ref[j] * cw[j:j + 1, :]
    act = acc * _sigmoid(acc)
    for i in range(3 * heads):
        sl = slice(i * LANES, (i + 1) * LANES)
        a = act[:, sl]
        if i < heads:
            a = _l2norm(a) * (LANES ** -0.5)
        elif i < 2 * heads:
            a = _l2norm(a)
        o_ref[:, sl] = a


def _gdn_step_kernel(s_ref, qc_ref, kc_ref, v_ref, z_ref, beta_ref, eg_ref, gn_ref, so_ref, o_ref, *, heads):
    gn = gn_ref[...]
    for h in range(heads):
        s = s_ref[0, h] * eg_ref[0, h]
        kc = kc_ref[0, h]
        kv = jnp.sum(kc * s, axis=0, keepdims=True)
        delta = (v_ref[0, h] - kv) * beta_ref[0, h]
        s = s + kc * delta
        so_ref[0, h] = s
        o = jnp.sum(qc_ref[0, h] * s, axis=0, keepdims=True)
        o = o * lax.rsqrt(jnp.mean(o * o, axis=-1, keepdims=True) + NORM_EPS) * gn
        z = z_ref[0, h]
        o_ref[0, h] = o * (z * _sigmoid(z))


def _gdn_sample(zg, bg, state_conv, s0, conv_w, gdn_norm_g, heads):
    bsz = s0.shape[0]
    mp = zg.shape[0]
    gw = heads * LANES
    x = zg[:, :3 * gw]
    ctx = jnp.pad(state_conv.transpose(1, 0, 2), ((0, 0), (0, mp - bsz), (0, 0)))
    qkv = pl.pallas_call(
        functools.partial(_gdn_sample_prep_kernel, heads=heads),
        out_shape=jax.ShapeDtypeStruct((mp, 3 * gw), F32),
        compiler_params=pltpu.CompilerParams(vmem_limit_bytes=VMEM_LIMIT),
        name="gdn_sample_prep",
    )(x, ctx, conv_w)
    qkv = qkv[:bsz]

    def col(a):
        return a.reshape(bsz, heads, LANES, 1)

    def row(a):
        return a.reshape(bsz, heads, 1, LANES)

    def lanes(a):
        return jnp.broadcast_to(a[:, :, None, None], (bsz, heads, 1, LANES))

    cspec = pl.BlockSpec((1, heads, LANES, 1), lambda i: (i, 0, 0, 0))
    rspec = pl.BlockSpec((1, heads, 1, LANES), lambda i: (i, 0, 0, 0))
    sspec = pl.BlockSpec((1, heads, LANES, LANES), lambda i: (i, 0, 0, 0))
    s_new, o = pl.pallas_call(
        functools.partial(_gdn_step_kernel, heads=heads),
        grid=(bsz,),
        in_specs=[sspec, cspec, cspec, rspec, rspec, rspec, rspec,
                  pl.BlockSpec((1, LANES), lambda i: (0, 0))],
        out_specs=[sspec, rspec],
        out_shape=[jax.ShapeDtypeStruct(s0.shape, F32),
                   jax.ShapeDtypeStruct((bsz, heads, 1, LANES), F32)],
        compiler_params=_params("parallel"),
        name="gdn_step",
    )(s0, col(qkv[:, :gw]), col(qkv[:, gw:2 * gw]), row(qkv[:, 2 * gw:]), row(zg[:bsz, 3 * gw:]),
      lanes(bg[:bsz, :heads]), lanes(bg[:bsz, 2 * heads:3 * heads]), gdn_norm_g.reshape(1, LANES))
    conv_new = jnp.concatenate([state_conv[:, 1:], x[:bsz, None, :]], axis=1)
    return o.reshape(bsz, gw), s_new, conv_new


def _suffix_ones():
    r = _iota2((SB_BLOCK, 2 * SB_BLOCK), 0)
    c = _iota2((SB_BLOCK, 2 * SB_BLOCK), 1)
    return jnp.where((r > c) | (c >= SB_BLOCK), 1.0, 0.0).astype(BF16)


def _sb_prompt_kernel(bias_ref, q_ref, k_ref, v_ref, o_ref):
    blk = SB_BLOCK
    i = pl.program_id(2)
    bias = bias_ref[pl.program_id(1)]
    q = q_ref[0]
    ucat = _suffix_ones()
    causal = _iota2((blk, blk), 1) < _iota2((blk, blk), 0)

    def block(j, masked, carry):
        later, acc = carry
        start = pl.multiple_of(j * blk, blk)
        kj = k_ref[0, pl.ds(start, blk), :]
        vj = v_ref[0, pl.ds(start, blk), :]
        z = lax.dot_general(q, kj, NT, preferred_element_type=F32) * (LANES ** -0.5) + bias
        sp = _softplus(z)
        lk = jnp.where(causal, -sp, 0.0) if masked else -sp
        hi, lo = _split(lk)
        cs = _dot(hi, ucat) + _dot(lo, ucat)
        wgt = jnp.exp(z - sp + cs[:, :blk] + later)
        if masked:
            wgt = jnp.where(causal, wgt, 0.0)
        acc = acc + _dot(wgt.astype(BF16), vj)
        return later + cs[:, blk:], acc

    zero = jnp.zeros((blk, blk), F32)
    carry = block(i, True, (zero, zero))
    carry = lax.fori_loop(0, i, lambda t, cr: block(i - 1 - t, False, cr), carry)
    o_ref[0] = carry[1].astype(o_ref.dtype)


def _sb_prompt(qkv, bias, heads):
    b, t, _ = qkv.shape
    blk = SB_BLOCK
    return pl.pallas_call(
        _sb_prompt_kernel,
        grid_spec=pltpu.PrefetchScalarGridSpec(
            num_scalar_prefetch=1,
            grid=(b, heads, t // blk),
            in_specs=[pl.BlockSpec((1, blk, LANES), lambda bi, h, i, s: (bi, i, h)),
                      pl.BlockSpec((1, t, LANES), lambda bi, h, i, s: (bi, 0, heads + h)),
                      pl.BlockSpec((1, t, LANES), lambda bi, h, i, s: (bi, 0, 2 * heads + h))],
            out_specs=pl.BlockSpec((1, blk, LANES), lambda bi, h, i, s: (bi, i, h)),
        ),
        out_shape=jax.ShapeDtypeStruct((b, t, heads * LANES), BF16),
        compiler_params=_params("parallel", "parallel", "arbitrary"),
        name="sb_prompt",
    )(bias, qkv, qkv, qkv)


def _sb_decode_kernel(pt_ref, q_ref, bias_ref, k_ref, v_ref, o_ref, qbd_ref, acc_ref, later_ref, *, heads):
    p = pl.program_id(1)

    @pl.when(p == 0)
    def _():
        acc_ref[...] = jnp.zeros_like(acc_ref)
        later_ref[...] = jnp.zeros_like(later_ref)
        qb = q_ref[0].astype(BF16)
        row = _iota2(qb.shape, 0)
        for h in range(heads):
            qbd_ref[:, h * LANES:(h + 1) * LANES] = jnp.where(row == h, qb, jnp.zeros_like(qb))

    kp = k_ref[0, 0].astype(BF16)
    vp = v_ref[0, 0].astype(BF16)
    z = lax.dot_general(qbd_ref[...], kp, NT, preferred_element_type=F32)
    z = z * (LANES ** -0.5) + bias_ref[...]
    sp = _softplus(z)
    hi, lo = _split(-sp)
    ucat = _suffix_ones()
    cs = _dot(hi, ucat) + _dot(lo, ucat)
    blk = SB_BLOCK
    wgt = jnp.exp(z - sp + cs[:, :blk] + later_ref[...])
    acc_ref[...] += _dot(wgt.astype(BF16), vp)
    later_ref[...] += cs[:, blk:]

    @pl.when(p == pl.num_programs(1) - 1)
    def _():
        for h in range(heads):
            o_ref[0, h:h + 1, :] = acc_ref[h:h + 1, h * LANES:(h + 1) * LANES]


def _sb_decode(q, cache_k, cache_v, layer, page_table, bias, heads):
    bsz, n_pages = page_table.shape
    page = cache_k.shape[2]
    assert page == SB_BLOCK
    sw = heads * LANES
    kv_spec = pl.BlockSpec((1, 1, page, sw), lambda b, p, pt: (layer, pt[b, n_pages - 1 - p], 0, 0))
    return pl.pallas_call(
        functools.partial(_sb_decode_kernel, heads=heads),
        grid_spec=pltpu.PrefetchScalarGridSpec(
            num_scalar_prefetch=1,
            grid=(bsz, n_pages),
            in_specs=[pl.BlockSpec((1, heads, LANES), lambda b, p, pt: (b, 0, 0)),
                      pl.BlockSpec((heads, 1), lambda b, p, pt: (0, 0)),
                      kv_spec, kv_spec],
            out_specs=pl.BlockSpec((1, heads, LANES), lambda b, p, pt: (b, 0, 0)),
            scratch_shapes=[pltpu.VMEM((heads, sw), BF16),
                            pltpu.VMEM((heads, sw), F32),
                            pltpu.VMEM((heads, SB_BLOCK), F32)],
        ),
        out_shape=jax.ShapeDtypeStruct((bsz, heads, LANES), F32),
        compiler_params=_params("parallel", "arbitrary"),
        name="sb_decode",
    )(page_table, q, bias.reshape(heads, 1), cache_k, cache_v)


def _ln_rows(x, g, b):
    mu = jnp.mean(x, axis=-1, keepdims=True)
    c = x - mu
    var = jnp.mean(c * c, axis=-1, keepdims=True)
    return c * lax.rsqrt(var + LN_EPS) * g + b


def _mlp_prompt_kernel(u_ref, v_ref, g_ref, b_ref, ws_ref, bst_ref, o_ref, *, groups):
    vn = _ln_rows(_gelu(v_ref[...]), g_ref[...], b_ref[...]).astype(BF16)
    tri = _iota2((MLP_CHUNK, MLP_CHUNK), 0) >= _iota2((MLP_CHUNK, MLP_CHUNK), 1)
    bst = bst_ref[...]
    for gi in range(groups):
        sl = slice(gi * LANES, (gi + 1) * LANES)
        ws = jnp.where(tri, ws_ref[gi], 0.0).astype(BF16)
        mixed = _dot(ws, vn[:, sl]) + bst[:, gi:gi + 1]
        o_ref[:, sl] = (_gelu(u_ref[:, sl]) * mixed).astype(o_ref.dtype)


def _mlp_prompt(uv, ln_g, ln_b, ws, bs):
    m = uv.shape[0]
    groups = ws.shape[0]
    mw = groups * LANES
    return pl.pallas_call(
        functools.partial(_mlp_prompt_kernel, groups=groups),
        grid=(m // MLP_CHUNK,),
        in_specs=[pl.BlockSpec((MLP_CHUNK, mw), lambda i: (i, 0)),
                  pl.BlockSpec((MLP_CHUNK, mw), lambda i: (i, 1)),
                  pl.BlockSpec((1, mw), lambda i: (0, 0)),
                  pl.BlockSpec((1, mw), lambda i: (0, 0)),
                  pl.BlockSpec((groups, MLP_CHUNK, MLP_CHUNK), lambda i: (0, 0, 0)),
                  pl.BlockSpec((MLP_CHUNK, groups), lambda i: (0, 0))],
        out_specs=pl.BlockSpec((MLP_CHUNK, mw), lambda i: (i, 0)),
        out_shape=jax.ShapeDtypeStruct((m, mw), BF16),
        compiler_params=_params("parallel"),
        name="mlp_prompt",
    )(uv, uv, ln_g.reshape(1, mw), ln_b.reshape(1, mw), ws, bs.T)


def _mlp_sample_kernel(uv_ref, g_ref, b_ref, w_ref, c_ref, vn_ref, o_ref):
    mw = g_ref.shape[1]
    vn = _ln_rows(_gelu(uv_ref[:, mw:]), g_ref[...], b_ref[...])
    vn_ref[...] = vn
    o_ref[...] = _gelu(uv_ref[:, :mw]) * (vn * w_ref[...] + c_ref[...])


def _mlp_sample(uv, ln_g, ln_b, ws, bs):
    m = uv.shape[0]
    mw = ws.shape[0] * LANES
    wvec = jnp.repeat(ws[:, 0, 0], LANES).reshape(1, mw)
    cvec = jnp.repeat(bs[:, 0], LANES).reshape(1, mw)
    return pl.pallas_call(
        _mlp_sample_kernel,
        out_shape=[jax.ShapeDtypeStruct((m, mw), F32), jax.ShapeDtypeStruct((m, mw), F32)],
        compiler_params=pltpu.CompilerParams(vmem_limit_bytes=VMEM_LIMIT),
        name="mlp_sample",
    )(uv, ln_g.reshape(1, mw), ln_b.reshape(1, mw), wvec, cvec)


def kernel(x_prompt, x_sample, state_gdn, state_conv, cache_k, cache_v, page_table, ln1_g, ln1_b, ffn1_wg, ffn1_wu, ffn1_wd, w_in, conv_w, a_log, dt_bias, gdn_norm_g, sb_bias, mlp_ln_g, mlp_ln_b, mlp_ws, mlp_bs, w_branch_a, w_branch_b, w_branch_c, w_out, ln2_g, ln2_b, ffn2_wg, ffn2_wu, ffn2_wd, ln3_g, ln3_b):
    depth = w_in.shape[0]
    bp, t, d = x_prompt.shape
    bs_, ts, _ = x_sample.shape
    assert ts == 1, "the sample group decodes one token per sequence"
    gh = a_log.shape[1]
    gw = gh * LANES
    sh = sb_bias.shape[1]
    sw = sh * LANES
    mw = mlp_ln_g.shape[1]
    assert gdn_norm_g.shape[1] == LANES and mlp_ws.shape[2] == MLP_CHUNK
    assert t % MLP_CHUNK == 0 and t % SB_BLOCK == 0 and t % GDN_CHUNK == 0
    assert 3 * gh <= LANES
    alpha = float((2 * depth) ** 0.25)
    mp = bp * t
    ms = BF16_ROWS * pl.cdiv(bs_, BF16_ROWS)

    n_pool, page = cache_k.shape[1], cache_k.shape[2]
    assert (page_table.shape[1] * page) % MLP_CHUNK == 0, "the new token must open a chunk"
    ck = cache_k.reshape(depth, n_pool, page, sw)
    cv = cache_v.reshape(depth, n_pool, page, sw)

    xp = x_prompt.reshape(mp, d)
    xs = jnp.pad(x_sample.reshape(bs_, d), ((0, ms - bs_), (0, 0)))
    xpb = xp.astype(BF16)
    xsb = xs.astype(BF16)

    o_z = 3 * gw
    o_b = o_z + gw
    o_sb = o_b + 2 * gh
    o_mlp = o_sb + 3 * sw
    o_gate = o_mlp + 2 * mw

    outs = [[] for _ in range(9)]
    for l in range(depth):
        bf = lambda a: a[l].astype(BF16)
        wg1, wu1, wd1 = bf(ffn1_wg), bf(ffn1_wu), bf(ffn1_wd)
        wg2, wu2, wd2 = bf(ffn2_wg), bf(ffn2_wu), bf(ffn2_wd)
        wa, wb, wc, wo = bf(w_branch_a), bf(w_branch_b), bf(w_branch_c), bf(w_out)
        win = bf(w_in)
        w_gdn = win[:, :o_b]
        w_beta, w_dec = win[:, o_b:o_b + gh], win[:, o_b + gh:o_sb]
        w_ba = jnp.concatenate([w_beta, w_dec, w_dec, jnp.zeros((d, LANES - 3 * gh), BF16)], axis=1)
        w_sb = win[:, o_sb:o_mlp]
        w_mlp = win[:, o_mlp:o_gate]
        w_gate = win[:, o_gate:]
        a_vec = jnp.pad(jnp.tile(a_log[l], 3), (0, LANES - 3 * gh)).reshape(1, LANES)
        dt_vec = jnp.pad(jnp.tile(dt_bias[l], 3), (0, LANES - 3 * gh)).reshape(1, LANES)

        def project(xb):
            zg = _mm(xb, w_gdn, name="in_gdn")
            bg = _mm(xb, w_ba, epilogue="gdn_gate", extras=(a_vec, dt_vec), n_heads=gh, name="in_gate_gdn")
            sb, sbb = _mm(xb, w_sb, out_dtypes=(F32, BF16), name="in_sb")
            uv = _mm(xb, w_mlp, name="in_mlp")
            gates = _mm(xb, w_gate, epilogue="sigmoid", name="in_gates")
            return zg, bg, sb, sbb, uv, gates

        def finish(x, oa, ob, oc, gates):
            merged = _merge(oa, ob, oc, wa, wb, wc, gates)
            r = _mm_res(merged, wo, x, alpha, 1.0, "out_proj")
            x, xb = _ln(r, ln2_g[l], ln2_b[l])
            return _ffn(x, xb, wg2, wu2, wd2, ln3_g[l], ln3_b[l], alpha)

        xp, xpb = _ffn(xp, xpb, wg1, wu1, wd1, ln1_g[l], ln1_b[l], alpha)
        zg, bg, sb, sbb, uv, gates = project(xpb)
        zg3 = zg.reshape(bp, t, 4 * gw)
        oa, s_fin = _gdn_prompt(zg3, bg.reshape(bp, t, LANES), conv_w[l], gdn_norm_g[l], gh)
        ob = _sb_prompt(sbb.reshape(bp, t, 3 * sw), sb_bias[l], sh)
        oc = _mlp_prompt(uv, mlp_ln_g[l], mlp_ln_b[l], mlp_ws[l], mlp_bs[l])
        xp, xpb = finish(xp, oa.reshape(mp, gw), ob.reshape(mp, sw), oc, gates)
        outs[0].append(s_fin)
        outs[1].append(zg3[:, t - 3:, :3 * gw])
        outs[2].append(sb[:, sw:2 * sw].reshape(bp, t, sh, LANES))
        outs[3].append(sb[:, 2 * sw:].reshape(bp, t, sh, LANES))

        xs, xsb = _ffn(xs, xsb, wg1, wu1, wd1, ln1_g[l], ln1_b[l], alpha)
        zg, bg, sb, sbb, uv, gates = project(xsb)
        oa, s_new, conv_new = _gdn_sample(zg, bg, state_conv[l], state_gdn[l], conv_w[l], gdn_norm_g[l], gh)
        ob = _sb_decode(sb[:bs_, :sw].reshape(bs_, sh, LANES), ck, cv, l, page_table, sb_bias[l], sh)
        vn, oc = _mlp_sample(uv, mlp_ln_g[l], mlp_ln_b[l], mlp_ws[l], mlp_bs[l])
        pad_rows = lambda a: jnp.pad(a.astype(BF16), ((0, ms - bs_), (0, 0)))
        xs, xsb = finish(xs, pad_rows(oa), pad_rows(ob.reshape(bs_, sw)), oc.astype(BF16), gates)
        outs[4].append(s_new)
        outs[5].append(conv_new)
        outs[6].append(sb[:bs_, sw:2 * sw].reshape(bs_, 1, sh, LANES))
        outs[7].append(sb[:bs_, 2 * sw:].reshape(bs_, 1, sh, LANES))
        outs[8].append(vn[:bs_].reshape(bs_, 1, mw))

    st = [jnp.stack(o) for o in outs]
    return (xp.reshape(bp, t, d), xs[:bs_].reshape(bs_, 1, d), st[0], st[1], st[2], st[3],
            st[4], st[5], st[6], st[7], st[8])
```

```python
import functools

import jax
import jax.numpy as jnp
from jax import lax
from jax.experimental import pallas as pl
from jax.experimental.pallas import tpu as pltpu

F32 = jnp.float32
BF16 = jnp.bfloat16

LANES = 128
SUBLANES = 8
BF16_ROWS = 16
VMEM_LIMIT = 56 * 1024 * 1024
LN_EPS = 1e-5
NORM_EPS = 1e-6
GDN_CHUNK = 64
GDN_HEADS_PER_STEP = 4
GDN_GROUPS_PER_STEP = 4
SB_BLOCK = 128
SB_QROWS = 512
MLP_CHUNK = 128
NT = (((1,), (1,)), ((), ()))
TN = (((0,), (0,)), ((), ()))


def _params(*sem):
    return pltpu.CompilerParams(dimension_semantics=sem, vmem_limit_bytes=VMEM_LIMIT)


def _tile(n, prefs):
    for t in prefs:
        if n % t == 0:
            return t
    return n


def _dot(a, b):
    return jnp.dot(a, b, preferred_element_type=F32)


def _sigmoid(x):
    return 1.0 / (1.0 + jnp.exp(-x))


def _softplus(x):
    return jnp.maximum(x, 0.0) + jnp.log1p(jnp.exp(-jnp.abs(x)))


def _gelu(x):
    return 0.5 * x * (1.0 + jnp.tanh(0.7978845608028654 * (x + 0.044715 * (x * x * x))))


def _split(a):
    hi = a.astype(BF16)
    lo = (a - hi.astype(F32)).astype(BF16)
    return hi, lo


def _dot_split(a, b):
    return _dot(a[0], b[0]) + _dot(a[0], b[1]) + _dot(a[1], b[0])


def _split3(a):
    hi = a.astype(BF16)
    r = a - hi.astype(F32)
    mid = r.astype(BF16)
    lo = (r - mid.astype(F32)).astype(BF16)
    return hi, mid, lo


def _iota2(shape, dim):
    return lax.broadcasted_iota(jnp.int32, shape, dim)


def _weights(w_refs, scratch):
    if not scratch:
        return [w[...] for w in w_refs]

    @pl.when(pl.program_id(1) == 0)
    def _():
        for w, s in zip(w_refs, scratch):
            s[...] = w[...].astype(BF16)

    return [s[...] for s in scratch]


def _w_spec(w, layer, k, tn, col0):
    assert col0 % tn == 0
    if w.ndim == 2:
        return pl.BlockSpec((k, tn), lambda j, i: (0, col0 // tn + j))
    return pl.BlockSpec((None, k, tn), lambda j, i: (layer, 0, col0 // tn + j))


def _w_scratch(ws, k, tn):
    if ws[0].dtype == BF16:
        return []
    return [pltpu.VMEM((k, tn), BF16) for _ in ws]


def _mm_kernel(x_ref, w_ref, *refs, epilogue, n_heads, n_extra, n_out):
    extras = refs[:n_extra]
    outs = refs[n_extra:n_extra + n_out]
    (w,) = _weights([w_ref], refs[n_extra + n_out:])
    acc = _dot(x_ref[...], w)
    if epilogue == "sigmoid":
        acc = _sigmoid(acc)
    elif epilogue == "gdn_gate":
        a_ref, dt_ref = extras
        lane = _iota2(acc.shape, 1)
        g = -jnp.exp(a_ref[...]) * _softplus(acc + dt_ref[...])
        acc = jnp.where(lane < n_heads, _sigmoid(acc), g)
    for o in outs:
        o[...] = acc.astype(o.dtype)


def _mm(x, w, n, *, layer=0, col0=0, out_dtypes=(F32,), epilogue=None, extras=(), n_heads=0, name="mm"):
    m, k = x.shape
    tm = _tile(m, (1024, 512, 256))
    tn = _tile(n, (512, 256, 128))
    outs = pl.pallas_call(
        functools.partial(_mm_kernel, epilogue=epilogue, n_heads=n_heads,
                          n_extra=len(extras), n_out=len(out_dtypes)),
        grid=(n // tn, m // tm),
        in_specs=[pl.BlockSpec((tm, k), lambda j, i: (i, 0)), _w_spec(w, layer, k, tn, col0)]
                 + [pl.BlockSpec((1, tn), lambda j, i: (0, j)) for _ in extras],
        out_specs=[pl.BlockSpec((tm, tn), lambda j, i: (i, j)) for _ in out_dtypes],
        out_shape=[jax.ShapeDtypeStruct((m, n), d) for d in out_dtypes],
        scratch_shapes=_w_scratch([w], k, tn),
        compiler_params=_params("arbitrary", "arbitrary"),
        name=name,
    )(x, w, *extras)
    return outs if len(out_dtypes) > 1 else outs[0]


def _ffn_up_kernel(x_ref, wg_ref, wu_ref, o_ref, *scratch):
    wg, wu = _weights([wg_ref, wu_ref], scratch)
    x = x_ref[...]
    g = _dot(x, wg)
    u = _dot(x, wu)
    o_ref[...] = (g * _sigmoid(g) * u).astype(o_ref.dtype)


def _ffn_up(x, wg, wu, layer):
    m, k = x.shape
    n = wg.shape[-1]
    tm = _tile(m, (1024, 512, 256))
    tn = _tile(n, (256, 128))
    return pl.pallas_call(
        _ffn_up_kernel,
        grid=(n // tn, m // tm),
        in_specs=[pl.BlockSpec((tm, k), lambda j, i: (i, 0)),
                  _w_spec(wg, layer, k, tn, 0), _w_spec(wu, layer, k, tn, 0)],
        out_specs=pl.BlockSpec((tm, tn), lambda j, i: (i, j)),
        out_shape=jax.ShapeDtypeStruct((m, n), BF16),
        scratch_shapes=_w_scratch([wg, wu], k, tn),
        compiler_params=_params("arbitrary", "arbitrary"),
        name="ffn_up",
    )(x, wg, wu)


def _mm_res_kernel(h_ref, w_ref, x_ref, o_ref, *scratch, alpha, scale):
    (w,) = _weights([w_ref], scratch)
    o_ref[...] = alpha * x_ref[...] + scale * _dot(h_ref[...], w)


def _mm_res(h, w, x, alpha, scale, name, layer=0):
    m, k = h.shape
    n = w.shape[-1]
    tn = _tile(n, (256, 128))
    if w.dtype == BF16:
        tm = _tile(m, (512, 256))
        grid = (m // tm, n // tn)
        ij = lambda a, b: (a, b)
    else:
        tm = _tile(m, (1024, 512, 256))
        grid = (n // tn, m // tm)
        ij = lambda a, b: (b, a)
    if w.ndim == 2:
        w_spec = pl.BlockSpec((k, tn), lambda a, b: (0, ij(a, b)[1]))
    else:
        w_spec = pl.BlockSpec((None, k, tn), lambda a, b: (layer, 0, ij(a, b)[1]))
    return pl.pallas_call(
        functools.partial(_mm_res_kernel, alpha=alpha, scale=scale),
        grid=grid,
        in_specs=[pl.BlockSpec((tm, k), lambda a, b: (ij(a, b)[0], 0)),
                  w_spec,
                  pl.BlockSpec((tm, tn), lambda a, b: ij(a, b))],
        out_specs=pl.BlockSpec((tm, tn), lambda a, b: ij(a, b)),
        out_shape=jax.ShapeDtypeStruct((m, n), F32),
        scratch_shapes=_w_scratch([w], k, tn),
        compiler_params=_params("arbitrary", "arbitrary"),
        name=name,
    )(h, w, x)


def _ln_kernel(r_ref, g_ref, b_ref, o_ref, ob_ref, *, rows):
    g = g_ref[...]
    b = b_ref[...]

    def body(i, carry):
        sl = pl.ds(pl.multiple_of(i * rows, rows), rows)
        r = r_ref[sl, :]
        mu = jnp.mean(r, axis=-1, keepdims=True)
        c = r - mu
        var = jnp.mean(c * c, axis=-1, keepdims=True)
        y = c * lax.rsqrt(var + LN_EPS) * g + b
        o_ref[sl, :] = y
        ob_ref[sl, :] = y.astype(BF16)
        return carry

    lax.fori_loop(0, r_ref.shape[0] // rows, body, 0)


def _ln(r, g, b):
    m, d = r.shape
    tr = _tile(m, (256,))
    rows = _tile(tr, (32, BF16_ROWS))
    return pl.pallas_call(
        functools.partial(_ln_kernel, rows=rows),
        grid=(m // tr,),
        in_specs=[pl.BlockSpec((tr, d), lambda i: (i, 0)),
                  pl.BlockSpec((1, d), lambda i: (0, 0)),
                  pl.BlockSpec((1, d), lambda i: (0, 0))],
        out_specs=[pl.BlockSpec((tr, d), lambda i: (i, 0)),
                   pl.BlockSpec((tr, d), lambda i: (i, 0))],
        out_shape=[jax.ShapeDtypeStruct((m, d), F32), jax.ShapeDtypeStruct((m, d), BF16)],
        compiler_params=_params("parallel"),
        name="layer_norm",
    )(r, g.reshape(1, d), b.reshape(1, d))


def _merge_kernel(oa_ref, ob_ref, oc_ref, wa_ref, wb_ref, wc_ref, ga_ref, gb_ref, gc_ref, o_ref, *scratch):
    wa, wb, wc = _weights([wa_ref, wb_ref, wc_ref], scratch)
    y = ga_ref[...] * _dot(oa_ref[...], wa)
    y = y + gb_ref[...] * _dot(ob_ref[...], wb)
    y = y + gc_ref[...] * _dot(oc_ref[...], wc)
    o_ref[...] = y.astype(o_ref.dtype)


def _merge(oa, ob, oc, wa, wb, wc, gates, layer):
    m, k = oa.shape
    d = wa.shape[-1]
    tm = _tile(m, (512, 256))
    tn = _tile(d, (256, 128))
    nb = d // tn
    o_spec = pl.BlockSpec((tm, k), lambda j, i: (i, 0))
    w_spec = _w_spec(wa, layer, k, tn, 0)
    return pl.pallas_call(
        _merge_kernel,
        grid=(nb, m // tm),
        in_specs=[o_spec, o_spec, o_spec, w_spec, w_spec, w_spec,
                  pl.BlockSpec((tm, tn), lambda j, i: (i, j)),
                  pl.BlockSpec((tm, tn), lambda j, i: (i, nb + j)),
                  pl.BlockSpec((tm, tn), lambda j, i: (i, 2 * nb + j))],
        out_specs=pl.BlockSpec((tm, tn), lambda j, i: (i, j)),
        out_shape=jax.ShapeDtypeStruct((m, d), BF16),
        scratch_shapes=_w_scratch([wa, wb, wc], k, tn),
        compiler_params=_params("arbitrary", "arbitrary"),
        name="branch_merge",
    )(oa, ob, oc, wa, wb, wc, gates, gates, gates)


def _causal_conv_silu(x, prev, cw):
    acc = x * cw[3:4, :]
    row = _iota2(prev.shape, 0)
    for s in (1, 2, 3):
        xs = pltpu.roll(x, s, 0)
        ps = pltpu.roll(prev, s, 0)
        head = jnp.where(row < s, ps, xs[:SUBLANES])
        xs = jnp.concatenate([head, xs[SUBLANES:]], axis=0)
        acc = acc + xs * cw[3 - s:4 - s, :]
    return acc * _sigmoid(acc)


def _l2norm(x):
    return x * lax.rsqrt(jnp.sum(x * x, axis=-1, keepdims=True) + NORM_EPS)


def _gdn_prep_kernel(xq_ref, xk_ref, xv_ref, pq_ref, pk_ref, pv_ref, cwq_ref, cwk_ref, cwv_ref,
                     beta_ref, g_ref, gt_ref,
                     u_ref, w_ref, qg_ref, kd_ref, qk_ref, eg_ref, *, hb, ng):
    has_prev = pl.program_id(1) > 0

    def conv(x_ref, p_ref, cw_ref):
        prev = jnp.where(has_prev, p_ref[0], 0.0)
        return _causal_conv_silu(x_ref[0], prev, cw_ref[...])

    qs = conv(xq_ref, pq_ref, cwq_ref)
    ks = conv(xk_ref, pk_ref, cwk_ref)
    vs = conv(xv_ref, pv_ref, cwv_ref)
    c = GDN_CHUNK
    n = hb * c
    groups = range(ng)

    def stack(x, gi):
        return jnp.concatenate([x[:, (gi * hb + h) * LANES:(gi * hb + h + 1) * LANES] for h in range(hb)], axis=0)

    def column(x, width=LANES):
        return jnp.concatenate(
            [jnp.broadcast_to(x[:, h:h + 1], (c, width)) for h in range(hb)], axis=0)

    r = _iota2((n, n), 0)
    cc = _iota2((n, n), 1)
    same = (r ^ cc) < c
    incl = same & (r >= cc)
    strict = same & (r > cc)
    eye = jnp.where(r == cc, 1.0, 0.0)
    r1 = _iota2((c, c), 0)
    c1 = _iota2((c, c), 1)
    tril = jnp.where(r1 >= c1, 1.0, 0.0).astype(BF16)
    triu = jnp.where(same & (r <= cc), 1.0, 0.0).astype(BF16)

    def setup(gi):
        q = _l2norm(stack(qs, gi)) * (LANES ** -0.5)
        k = _l2norm(stack(ks, gi))
        v = stack(vs, gi)
        g3 = _split3(g_ref[0, gi])
        gcw = column(_dot(tril, g3[0]) + _dot(tril, g3[1]) + _dot(tril, g3[2]), n)
        gcc = gcw[:, :LANES]
        t3 = _split3(gt_ref[0, gi, 0])
        gcr = (_dot(t3[0], triu) + _dot(t3[1], triu) + _dot(t3[2], triu))[0:1, :]
        beta = column(beta_ref[0, gi])
        decay = jnp.where(incl, jnp.exp(jnp.minimum(gcw - gcr, 0.0)), 0.0)
        kb = k * beta
        kbf = k.astype(BF16)
        kk = lax.dot_general(kb.astype(BF16), kbf, NT, preferred_element_type=F32)
        lmat = jnp.where(strict, kk * decay, 0.0)
        egc = jnp.exp(gcc)
        rhs = jnp.concatenate([v * beta, kb * egc], axis=1)
        qk = lax.dot_general(q.astype(BF16), kbf, NT, preferred_element_type=F32) * decay
        return dict(q=q, k=k, gcc=gcc, egc=egc, lmat=lmat, rhs=rhs, qk=qk)

    st = [setup(gi) for gi in groups]
    tinv = [eye - s["lmat"] for s in st]
    p = [_split(s["lmat"]) for s in st]
    for level in range(5):
        p = [_split(_dot_split(pg, pg)) for pg in p]
        tinv = [tg + _dot_split(_split(tg), pg) for tg, pg in zip(tinv, p)]
    sol = [_dot_split(_split(tg), _split(s["rhs"])) for tg, s in zip(tinv, st)]

    for gi in groups:
        s = st[gi]
        gcc = s["gcc"]
        glast = jnp.concatenate(
            [jnp.broadcast_to(gcc[(h + 1) * c - 1:(h + 1) * c, :], (c, LANES)) for h in range(hb)], axis=0)
        qg = (s["q"] * s["egc"]).astype(BF16)
        kd = (s["k"] * jnp.exp(glast - gcc)).astype(BF16)
        for h in range(hb):
            rows = slice(h * c, (h + 1) * c)
            hd = gi * hb + h
            u_ref[0, 0, hd] = sol[gi][rows, :LANES]
            w_ref[0, 0, hd] = sol[gi][rows, LANES:].astype(BF16)
            qg_ref[0, 0, hd] = qg[rows]
            kd_ref[0, 0, hd] = kd[rows]
            vcol = (h * c) // LANES
            blk = s["qk"][rows, vcol * LANES:(vcol + 1) * LANES]
            if (h * c) % LANES:
                blk = pltpu.roll(blk, LANES - (h * c) % LANES, 1)
            qk_ref[0, 0, hd] = blk[:, :c].astype(BF16)
            eg_ref[0, 0, hd] = jnp.exp(gcc[(h + 1) * c - 1:(h + 1) * c, :])


def _gdn_scan_kernel(u_ref, w_ref, qg_ref, kd_ref, qk_ref, eg_ref, z_ref, gn_ref, o_ref, s_ref, *, heads):
    @pl.when(pl.program_id(1) == 0)
    def _():
        s_ref[...] = jnp.zeros_like(s_ref)

    gn = gn_ref[...]
    for h in range(heads):
        s = s_ref[0, h]
        sb = s.astype(BF16)
        v_new = u_ref[0, 0, h] - _dot(w_ref[0, 0, h], sb)
        vb = v_new.astype(BF16)
        o = _dot(qg_ref[0, 0, h], sb) + _dot(qk_ref[0, 0, h], vb)
        s_ref[0, h] = s * eg_ref[0, 0, h] + lax.dot_general(kd_ref[0, 0, h], vb, TN,
                                                           preferred_element_type=F32)
        o = o * lax.rsqrt(jnp.mean(o * o, axis=-1, keepdims=True) + NORM_EPS) * gn
        z = z_ref[0, :, h * LANES:(h + 1) * LANES]
        o_ref[0, :, h * LANES:(h + 1) * LANES] = (o * (z * _sigmoid(z))).astype(o_ref.dtype)


def _gdn_prompt(zg, bg, conv_w, gdn_norm_g, heads):
    b, t, _ = zg.shape
    c = GDN_CHUNK
    n = t // c
    hb = GDN_HEADS_PER_STEP if heads % GDN_HEADS_PER_STEP == 0 else 1
    hg = heads // hb
    ng = GDN_GROUPS_PER_STEP if hg % GDN_GROUPS_PER_STEP == 0 else 1
    gw = heads * LANES
    wb = ng * hb * LANES
    nq = gw // wb

    def regroup(a):
        a = a.reshape(b, t, hg, hb).transpose(0, 2, 1, 3)
        return jnp.pad(a, ((0, 0), (0, 0), (0, 0), (0, LANES - hb)))

    beta = regroup(bg[:, :, :heads])
    g = bg[:, :, heads:2 * heads]
    gcol = regroup(g)
    gt = g.reshape(b, n, c, hg, hb).transpose(0, 3, 1, 4, 2).reshape(b, hg, n, 1, hb * c)
    gt = jnp.pad(gt, ((0, 0), (0, 0), (0, 0), (0, SUBLANES - 1), (0, 0)))

    def xspec(off):
        return pl.BlockSpec((1, c, wb), lambda i, j, h: (i, j, off + h))

    def pspec(off):
        return pl.BlockSpec((1, SUBLANES, wb),
                            lambda i, j, h: (i, jnp.maximum(j * (c // SUBLANES) - 1, 0), off + h))

    def cspec(off):
        return pl.BlockSpec((None, 4, wb), lambda i, j, h: (conv_w[1], 0, off + h))

    def ospec(rows, cols):
        return pl.BlockSpec((1, 1, ng * hb, rows, cols), lambda i, j, h: (i, j, h, 0, 0))

    def oshape(rows, cols, dt):
        return jax.ShapeDtypeStruct((b, n, heads, rows, cols), dt)

    cw = conv_w[0]
    u, w, qg, kd, qk, eg = pl.pallas_call(
        functools.partial(_gdn_prep_kernel, hb=hb, ng=ng),
        grid=(b, n, hg // ng),
        in_specs=[xspec(0), xspec(nq), xspec(2 * nq), pspec(0), pspec(nq), pspec(2 * nq),
                  cspec(0), cspec(nq), cspec(2 * nq),
                  pl.BlockSpec((1, ng, c, LANES), lambda i, j, h: (i, h, j, 0)),
                  pl.BlockSpec((1, ng, c, LANES), lambda i, j, h: (i, h, j, 0)),
                  pl.BlockSpec((1, ng, 1, SUBLANES, hb * c), lambda i, j, h: (i, h, j, 0, 0))],
        out_specs=[ospec(c, LANES), ospec(c, LANES), ospec(c, LANES), ospec(c, LANES),
                   ospec(c, c), ospec(1, LANES)],
        out_shape=[oshape(c, LANES, F32), oshape(c, LANES, BF16), oshape(c, LANES, BF16),
                   oshape(c, LANES, BF16), oshape(c, c, BF16), oshape(1, LANES, F32)],
        compiler_params=_params("parallel", "parallel", "parallel"),
        name="gdn_prep",
    )(zg, zg, zg, zg, zg, zg, cw, cw, cw, beta, gcol, gt)

    def sspec(rows, cols):
        return pl.BlockSpec((1, 1, heads, rows, cols), lambda i, j: (i, j, 0, 0, 0))

    o, s_fin = pl.pallas_call(
        functools.partial(_gdn_scan_kernel, heads=heads),
        grid=(b, n),
        in_specs=[sspec(c, LANES), sspec(c, LANES), sspec(c, LANES), sspec(c, LANES),
                  sspec(c, c), sspec(1, LANES),
                  pl.BlockSpec((1, c, gw), lambda i, j: (i, j, 3)),
                  pl.BlockSpec((1, LANES), lambda i, j: (0, 0))],
        out_specs=[pl.BlockSpec((1, c, gw), lambda i, j: (i, j, 0)),
                   pl.BlockSpec((1, heads, LANES, LANES), lambda i, j: (i, 0, 0, 0))],
        out_shape=[jax.ShapeDtypeStruct((b, t, gw), BF16),
                   jax.ShapeDtypeStruct((b, heads, LANES, LANES), F32)],
        compiler_params=_params("parallel", "arbitrary"),
        name="gdn_scan",
    )(u, w, qg, kd, qk, eg, zg, gdn_norm_g.reshape(1, LANES))
    return o, s_fin


def _gdn_sample_prep_kernel(x_ref, ctx_ref, cw_ref, o_ref, *, heads):
    cw = cw_ref[...]
    acc = x_ref[...] * cw[3:4, :]
    for j in range(3):
        acc = acc + ctx_ref[j] * cw[j:j + 1, :]
    act = acc * _sigmoid(acc)
    for i in range(3 * heads):
        sl = slice(i * LANES, (i + 1) * LANES)
        a = act[:, sl]
        if i < heads:
            a = _l2norm(a) * (LANES ** -0.5)
        elif i < 2 * heads:
            a = _l2norm(a)
        o_ref[:, sl] = a


def _gdn_step_kernel(s_ref, qc_ref, kc_ref, v_ref, z_ref, beta_ref, g_ref, gn_ref, so_ref, o_ref, *, heads):
    gn = gn_ref[...]
    for h in range(heads):
        s = s_ref[0, h] * jnp.exp(g_ref[0, h])
        kc = kc_ref[0, h]
        kv = jnp.sum(kc * s, axis=0, keepdims=True)
        delta = (v_ref[0, h] - kv) * beta_ref[0, h]
        s = s + kc * delta
        so_ref[0, h] = s
        o = jnp.sum(qc_ref[0, h] * s, axis=0, keepdims=True)
        o = o * lax.rsqrt(jnp.mean(o * o, axis=-1, keepdims=True) + NORM_EPS) * gn
        z = z_ref[0, h]
        o_ref[0, h] = o * (z * _sigmoid(z))


def _gdn_sample(zg, bg, state_conv, s0, conv_w, gdn_norm_g, heads):
    bsz = s0.shape[0]
    mp = zg.shape[0]
    gw = heads * LANES
    x = zg[:, :3 * gw]
    ctx = jnp.pad(state_conv.transpose(1, 0, 2), ((0, 0), (0, mp - bsz), (0, 0)))
    qkv = pl.pallas_call(
        functools.partial(_gdn_sample_prep_kernel, heads=heads),
        out_shape=jax.ShapeDtypeStruct((mp, 3 * gw), F32),
        compiler_params=pltpu.CompilerParams(vmem_limit_bytes=VMEM_LIMIT),
        name="gdn_sample_prep",
    )(x, ctx, conv_w[0][conv_w[1]])
    qkv = qkv[:bsz]

    def col(a):
        return a.reshape(bsz, heads, LANES, 1)

    def row(a):
        return a.reshape(bsz, heads, 1, LANES)

    def lanes(a):
        return jnp.broadcast_to(a[:, :, None, None], (bsz, heads, 1, LANES))

    cspec = pl.BlockSpec((1, heads, LANES, 1), lambda i: (i, 0, 0, 0))
    rspec = pl.BlockSpec((1, heads, 1, LANES), lambda i: (i, 0, 0, 0))
    sspec = pl.BlockSpec((1, heads, LANES, LANES), lambda i: (i, 0, 0, 0))
    s_new, o = pl.pallas_call(
        functools.partial(_gdn_step_kernel, heads=heads),
        grid=(bsz,),
        in_specs=[sspec, cspec, cspec, rspec, rspec, rspec, rspec,
                  pl.BlockSpec((1, LANES), lambda i: (0, 0))],
        out_specs=[sspec, rspec],
        out_shape=[jax.ShapeDtypeStruct(s0.shape, F32),
                   jax.ShapeDtypeStruct((bsz, heads, 1, LANES), F32)],
        compiler_params=_params("parallel"),
        name="gdn_step",
    )(s0, col(qkv[:, :gw]), col(qkv[:, gw:2 * gw]), row(qkv[:, 2 * gw:]), row(zg[:bsz, 3 * gw:]),
      lanes(bg[:bsz, :heads]), lanes(bg[:bsz, heads:2 * heads]), gdn_norm_g.reshape(1, LANES))
    conv_new = jnp.concatenate([state_conv[:, 1:], x[:bsz, None, :]], axis=1)
    return o.reshape(bsz, gw), s_new, conv_new


def _suffix_ones():
    r = _iota2((SB_BLOCK, 2 * SB_BLOCK), 0)
    c = _iota2((SB_BLOCK, 2 * SB_BLOCK), 1)
    return jnp.where((r > c) | (c >= SB_BLOCK), 1.0, 0.0).astype(BF16)


def _sb_prompt_kernel(bias_ref, q_ref, k_ref, v_ref, o_ref, later_ref, acc_ref, *, qrows):
    blk = SB_BLOCK
    nsub = qrows // blk
    first = pl.program_id(2) * nsub
    bias = bias_ref[pl.program_id(1)]
    ucat = _suffix_ones()
    later_ref[...] = jnp.zeros_like(later_ref)
    acc_ref[...] = jnp.zeros_like(acc_ref)

    def block(j, r0, dj):
        start = pl.multiple_of(j * blk, blk)
        kj = k_ref[0, pl.ds(start, blk), :]
        vj = v_ref[0, pl.ds(start, blk), :]
        z = lax.dot_general(q_ref[0, r0:, :], kj, NT, preferred_element_type=F32) * (LANES ** -0.5) + bias
        sp = jnp.maximum(z, 0.0) + jnp.log(1.0 + jnp.exp(-jnp.abs(z)))
        lk = -sp
        if dj is not None:
            causal = (_iota2(z.shape, 1) + dj * blk) < (_iota2(z.shape, 0) + r0)
            lk = jnp.where(causal, lk, 0.0)
        hi, lo = _split(lk)
        cs = _dot(hi, ucat) + _dot(lo, ucat)
        wgt = jnp.exp(z - sp + cs[:, :blk] + later_ref[r0:, :])
        if dj is not None:
            wgt = jnp.where(causal, wgt, 0.0)
        acc_ref[r0:, :] += _dot(wgt.astype(BF16), vj)
        later_ref[r0:, :] += cs[:, blk:]

    for dj in reversed(range(nsub)):
        block(first + dj, dj * blk, dj)

    def pair(t, carry):
        block(first - 1 - 2 * t, 0, None)
        block(first - 2 - 2 * t, 0, None)
        return carry

    if nsub % 2 == 0:
        lax.fori_loop(0, first // 2, pair, 0)
    else:
        lax.fori_loop(0, first, lambda t, cr: (block(first - 1 - t, 0, None), cr)[1], 0)
    o_ref[0] = acc_ref[...].astype(o_ref.dtype)


def _sb_prompt(qkv, bias, heads):
    b, t, _ = qkv.shape
    qrows = _tile(t, (SB_QROWS, 2 * SB_BLOCK, SB_BLOCK))
    return pl.pallas_call(
        functools.partial(_sb_prompt_kernel, qrows=qrows),
        grid_spec=pltpu.PrefetchScalarGridSpec(
            num_scalar_prefetch=1,
            grid=(b, heads, t // qrows),
            in_specs=[pl.BlockSpec((1, qrows, LANES), lambda bi, h, i, s: (bi, i, h)),
                      pl.BlockSpec((1, t, LANES), lambda bi, h, i, s: (bi, 0, heads + h)),
                      pl.BlockSpec((1, t, LANES), lambda bi, h, i, s: (bi, 0, 2 * heads + h))],
            out_specs=pl.BlockSpec((1, qrows, LANES), lambda bi, h, i, s: (bi, i, h)),
            scratch_shapes=[pltpu.VMEM((qrows, LANES), F32), pltpu.VMEM((qrows, LANES), F32)],
        ),
        out_shape=jax.ShapeDtypeStruct((b, t, heads * LANES), BF16),
        compiler_params=_params("parallel", "parallel", "arbitrary"),
        name="sb_prompt",
    )(bias, qkv, qkv, qkv)


def _sb_decode_kernel(pt_ref, q_ref, bias_ref, k_ref, v_ref, o_ref, qbd_ref, acc_ref, later_ref, *, heads):
    p = pl.program_id(1)
    page = SB_BLOCK

    @pl.when(p == 0)
    def _():
        acc_ref[...] = jnp.zeros_like(acc_ref)
        later_ref[...] = jnp.zeros_like(later_ref)
        qb = q_ref[0].astype(BF16)
        row = _iota2(qb.shape, 0)
        for h in range(heads):
            qbd_ref[:, h * LANES:(h + 1) * LANES] = jnp.where(row == h, qb, jnp.zeros_like(qb))

    def by_position(ref):
        return jnp.concatenate(
            [ref[0, 0, pl.ds(h, page, stride=heads), :] for h in range(heads)], axis=1).astype(BF16)

    z = lax.dot_general(qbd_ref[...], by_position(k_ref), NT, preferred_element_type=F32)
    z = z * (LANES ** -0.5) + bias_ref[...]
    sp = _softplus(z)
    hi, lo = _split(-sp)
    ucat = _suffix_ones()
    cs = _dot(hi, ucat) + _dot(lo, ucat)
    wgt = jnp.exp(z - sp + cs[:, :page] + later_ref[...])
    acc_ref[...] += _dot(wgt.astype(BF16), by_position(v_ref))
    later_ref[...] += cs[:, page:]

    @pl.when(p == pl.num_programs(1) - 1)
    def _():
        for h in range(heads):
            o_ref[0, h:h + 1, :] = acc_ref[h:h + 1, h * LANES:(h + 1) * LANES]


def _sb_decode(q, cache_k, cache_v, layer, page_table, bias, heads):
    bsz, n_pages = page_table.shape
    rows = cache_k.shape[2]
    assert rows == SB_BLOCK * heads and heads & (heads - 1) == 0
    kv_spec = pl.BlockSpec((1, 1, rows, LANES), lambda b, p, pt: (layer, pt[b, n_pages - 1 - p], 0, 0))
    return pl.pallas_call(
        functools.partial(_sb_decode_kernel, heads=heads),
        grid_spec=pltpu.PrefetchScalarGridSpec(
            num_scalar_prefetch=1,
            grid=(bsz, n_pages),
            in_specs=[pl.BlockSpec((1, heads, LANES), lambda b, p, pt: (b, 0, 0)),
                      pl.BlockSpec((heads, 1), lambda b, p, pt: (0, 0)),
                      kv_spec, kv_spec],
            out_specs=pl.BlockSpec((1, heads, LANES), lambda b, p, pt: (b, 0, 0)),
            scratch_shapes=[pltpu.VMEM((heads, heads * LANES), BF16),
                            pltpu.VMEM((heads, heads * LANES), F32),
                            pltpu.VMEM((heads, SB_BLOCK), F32)],
        ),
        out_shape=jax.ShapeDtypeStruct((bsz, heads, LANES), F32),
        compiler_params=_params("parallel", "arbitrary"),
        name="sb_decode",
    )(page_table, q, bias.reshape(heads, 1), cache_k, cache_v)


def _ln_rows(x, g, b):
    mu = jnp.mean(x, axis=-1, keepdims=True)
    c = x - mu
    var = jnp.mean(c * c, axis=-1, keepdims=True)
    return c * lax.rsqrt(var + LN_EPS) * g + b


def _mlp_prompt_kernel(u_ref, v_ref, g_ref, b_ref, ws_ref, bst_ref, o_ref, *, groups):
    vn = _ln_rows(_gelu(v_ref[...]), g_ref[...], b_ref[...]).astype(BF16)
    tri = _iota2((MLP_CHUNK, MLP_CHUNK), 0) >= _iota2((MLP_CHUNK, MLP_CHUNK), 1)
    bst = bst_ref[...]
    for gi in range(groups):
        sl = slice(gi * LANES, (gi + 1) * LANES)
        ws = jnp.where(tri, ws_ref[gi], 0.0).astype(BF16)
        mixed = _dot(ws, vn[:, sl]) + bst[:, gi:gi + 1]
        o_ref[:, sl] = (_gelu(u_ref[:, sl]) * mixed).astype(o_ref.dtype)


def _mlp_prompt(uv, ln_g, ln_b, ws, bs):
    m = uv.shape[0]
    groups = ws.shape[0]
    mw = groups * LANES
    return pl.pallas_call(
        functools.partial(_mlp_prompt_kernel, groups=groups),
        grid=(m // MLP_CHUNK,),
        in_specs=[pl.BlockSpec((MLP_CHUNK, mw), lambda i: (i, 0)),
                  pl.BlockSpec((MLP_CHUNK, mw), lambda i: (i, 1)),
                  pl.BlockSpec((1, mw), lambda i: (0, 0)),
                  pl.BlockSpec((1, mw), lambda i: (0, 0)),
                  pl.BlockSpec((groups, MLP_CHUNK, MLP_CHUNK), lambda i: (0, 0, 0)),
                  pl.BlockSpec((MLP_CHUNK, groups), lambda i: (0, 0))],
        out_specs=pl.BlockSpec((MLP_CHUNK, mw), lambda i: (i, 0)),
        out_shape=jax.ShapeDtypeStruct((m, mw), BF16),
        compiler_params=_params("parallel"),
        name="mlp_prompt",
    )(uv, uv, ln_g.reshape(1, mw), ln_b.reshape(1, mw), ws, bs.T)


def _mlp_sample_kernel(uv_ref, g_ref, b_ref, w_ref, c_ref, vn_ref, o_ref):
    mw = g_ref.shape[1]
    vn = _ln_rows(_gelu(uv_ref[:, mw:]), g_ref[...], b_ref[...])
    vn_ref[...] = vn
    o_ref[...] = _gelu(uv_ref[:, :mw]) * (vn * w_ref[...] + c_ref[...])


def _mlp_sample(uv, ln_g, ln_b, ws, bs):
    m = uv.shape[0]
    mw = ws.shape[0] * LANES
    wvec = jnp.repeat(ws[:, 0, 0], LANES).reshape(1, mw)
    cvec = jnp.repeat(bs[:, 0], LANES).reshape(1, mw)
    return pl.pallas_call(
        _mlp_sample_kernel,
        out_shape=[jax.ShapeDtypeStruct((m, mw), F32), jax.ShapeDtypeStruct((m, mw), F32)],
        compiler_params=pltpu.CompilerParams(vmem_limit_bytes=VMEM_LIMIT),
        name="mlp_sample",
    )(uv, ln_g.reshape(1, mw), ln_b.reshape(1, mw), wvec, cvec)


def kernel(x_prompt, x_sample, state_gdn, state_conv, cache_k, cache_v, page_table, ln1_g, ln1_b, ffn1_wg, ffn1_wu, ffn1_wd, w_in, conv_w, a_log, dt_bias, gdn_norm_g, sb_bias, mlp_ln_g, mlp_ln_b, mlp_ws, mlp_bs, w_branch_a, w_branch_b, w_branch_c, w_out, ln2_g, ln2_b, ffn2_wg, ffn2_wu, ffn2_wd, ln3_g, ln3_b):
    depth = w_in.shape[0]
    bp, t, d = x_prompt.shape
    bs_, ts, _ = x_sample.shape
    assert ts == 1, "the sample group decodes one token per sequence"
    gh = a_log.shape[1]
    gw = gh * LANES
    sh = sb_bias.shape[1]
    sw = sh * LANES
    mw = mlp_ln_g.shape[1]
    assert gdn_norm_g.shape[1] == LANES and mlp_ws.shape[2] == MLP_CHUNK
    assert t % MLP_CHUNK == 0 and t % SB_BLOCK == 0 and t % GDN_CHUNK == 0
    assert 2 * gh <= LANES
    alpha = float((2 * depth) ** 0.25)
    mp = bp * t
    ms = BF16_ROWS * pl.cdiv(bs_, BF16_ROWS)

    n_pool, page = cache_k.shape[1], cache_k.shape[2]
    assert page == SB_BLOCK
    assert (page_table.shape[1] * page) % MLP_CHUNK == 0, "the new token must open a chunk"
    ck = cache_k.reshape(depth, n_pool, page * sh, LANES)
    cv = cache_v.reshape(depth, n_pool, page * sh, LANES)

    xp = x_prompt.reshape(mp, d)
    xs = jnp.pad(x_sample.reshape(bs_, d), ((0, ms - bs_), (0, 0)))
    xpb = xp.astype(BF16)
    xsb = xs.astype(BF16)

    o_gate_gdn = 4 * gw
    o_sb = o_gate_gdn + 2 * gh
    pad_lane = lambda a: jnp.pad(a, (gh, LANES - 2 * gh)).reshape(1, LANES)

    outs = [[] for _ in range(9)]
    for l in range(depth):
        wd1 = ffn1_wd[l].astype(BF16)
        wd2 = ffn2_wd[l].astype(BF16)
        w_rest = w_in[l][:, o_sb:].astype(BF16)
        a_vec, dt_vec = pad_lane(a_log[l]), pad_lane(dt_bias[l])

        def ffn(x, xb, wg, wu, wd, g, b):
            h = _ffn_up(xb, wg, wu, l)
            return _ln(_mm_res(h, wd, x, alpha, 0.5, "ffn_down"), g, b)

        def project(xb):
            zg = _mm(xb, w_in, 4 * gw, layer=l, name="in_gdn")
            bg = _mm(xb, w_in, LANES, layer=l, col0=o_gate_gdn, epilogue="gdn_gate",
                     extras=(a_vec, dt_vec), n_heads=gh, name="in_gate_gdn")
            sb, sbb = _mm(xb, w_rest, 3 * sw, out_dtypes=(F32, BF16), name="in_sb")
            uv = _mm(xb, w_rest, 2 * mw, col0=3 * sw, name="in_mlp")
            gates = _mm(xb, w_rest, 3 * d, col0=3 * sw + 2 * mw, epilogue="sigmoid", name="in_gates")
            return zg, bg, sb, sbb, uv, gates

        def finish(x, oa, ob, oc, gates):
            merged = _merge(oa, ob, oc, w_branch_a, w_branch_b, w_branch_c, gates, l)
            r = _mm_res(merged, w_out, x, alpha, 1.0, "out_proj", layer=l)
            x, xb = _ln(r, ln2_g[l], ln2_b[l])
            return ffn(x, xb, ffn2_wg, ffn2_wu, wd2, ln3_g[l], ln3_b[l])

        xp, xpb = ffn(xp, xpb, ffn1_wg, ffn1_wu, wd1, ln1_g[l], ln1_b[l])
        zg, bg, sb, sbb, uv, gates = project(xpb)
        zg3 = zg.reshape(bp, t, 4 * gw)
        oa, s_fin = _gdn_prompt(zg3, bg.reshape(bp, t, LANES), (conv_w, l), gdn_norm_g[l], gh)
        ob = _sb_prompt(sbb.reshape(bp, t, 3 * sw), sb_bias[l], sh)
        oc = _mlp_prompt(uv, mlp_ln_g[l], mlp_ln_b[l], mlp_ws[l], mlp_bs[l])
        xp, xpb = finish(xp, oa.reshape(mp, gw), ob.reshape(mp, sw), oc, gates)
        outs[0].append(s_fin)
        outs[1].append(zg3[:, t - 3:, :3 * gw])
        outs[2].append(sb[:, sw:2 * sw].reshape(bp, t, sh, LANES))
        outs[3].append(sb[:, 2 * sw:].reshape(bp, t, sh, LANES))

        xs, xsb = ffn(xs, xsb, ffn1_wg, ffn1_wu, wd1, ln1_g[l], ln1_b[l])
        zg, bg, sb, sbb, uv, gates = project(xsb)
        oa, s_new, conv_new = _gdn_sample(zg, bg, state_conv[l], state_gdn[l], (conv_w, l), gdn_norm_g[l], gh)
        ob = _sb_decode(sb[:bs_, :sw].reshape(bs_, sh, LANES), ck, cv, l, page_table, sb_bias[l], sh)
        vn, oc = _mlp_sample(uv, mlp_ln_g[l], mlp_ln_b[l], mlp_ws[l], mlp_bs[l])
        pad_rows = lambda a: jnp.pad(a.astype(BF16), ((0, ms - bs_), (0, 0)))
        xs, xsb = finish(xs, pad_rows(oa), pad_rows(ob.reshape(bs_, sw)), oc.astype(BF16), gates)
        outs[4].append(s_new)
        outs[5].append(conv_new)
        outs[6].append(sb[:bs_, sw:2 * sw].reshape(bs_, 1, sh, LANES))
        outs[7].append(sb[:bs_, 2 * sw:].reshape(bs_, 1, sh, LANES))
        outs[8].append(vn[:bs_].reshape(bs_, 1, mw))

    st = [jnp.stack(o) for o in outs]
    return (xp.reshape(bp, t, d), xs[:bs_].reshape(bs_, 1, d), st[0], st[1], st[2], st[3],
            st[4], st[5], st[6], st[7], st[8])
```

```python
import functools

import jax
import jax.numpy as jnp
from jax import lax
from jax.experimental import pallas as pl
from jax.experimental.pallas import tpu as pltpu

F32 = jnp.float32
BF16 = jnp.bfloat16

LANES = 128
SUBLANES = 8
BF16_ROWS = 16
VMEM_LIMIT = 56 * 1024 * 1024
LN_EPS = 1e-5
NORM_EPS = 1e-6
GDN_CHUNK = 64
GDN_HEADS_PER_STEP = 4
GDN_GROUPS_PER_STEP = 4
SB_BLOCK = 128
SB_QROWS = 512
SB_DECODE_PAGES = 4
MLP_CHUNK = 128
NT = (((1,), (1,)), ((), ()))
TN = (((0,), (0,)), ((), ()))


def _params(*sem):
    return pltpu.CompilerParams(dimension_semantics=sem, vmem_limit_bytes=VMEM_LIMIT)


def _tile(n, prefs):
    for t in prefs:
        if n % t == 0:
            return t
    return n


def _dot(a, b):
    return jnp.dot(a, b, preferred_element_type=F32)


def _sigmoid(x):
    return 1.0 / (1.0 + jnp.exp(-x))


def _softplus(x):
    return jnp.maximum(x, 0.0) + jnp.log1p(jnp.exp(-jnp.abs(x)))


def _gelu(x):
    return 0.5 * x * (1.0 + jnp.tanh(0.7978845608028654 * (x + 0.044715 * (x * x * x))))


def _split(a):
    hi = a.astype(BF16)
    lo = (a - hi.astype(F32)).astype(BF16)
    return hi, lo


def _dot_split(a, b):
    return _dot(a[0], b[0]) + _dot(a[0], b[1]) + _dot(a[1], b[0])


def _split3(a):
    hi = a.astype(BF16)
    r = a - hi.astype(F32)
    mid = r.astype(BF16)
    lo = (r - mid.astype(F32)).astype(BF16)
    return hi, mid, lo


def _iota2(shape, dim):
    return lax.broadcasted_iota(jnp.int32, shape, dim)


def _weights(w_refs, scratch, first):
    if not scratch:
        return [w[...] for w in w_refs]

    @pl.when(first)
    def _():
        for w, s in zip(w_refs, scratch):
            s[...] = w[...].astype(BF16)

    return [s[...] for s in scratch]


def _w_spec(w, layer, k, tn, col0=0, transposed=False):
    if transposed:
        assert col0 % SUBLANES == 0 and tn % SUBLANES == 0
        return pl.BlockSpec((pl.Element(1), pl.Element(tn), pl.Element(k)),
                            lambda j, i: (layer, pl.multiple_of(col0 + j * tn, SUBLANES), 0))
    assert col0 % tn == 0
    return pl.BlockSpec((None, k, tn), lambda j, i: (layer, 0, col0 // tn + j))


def _w_scratch(ws, k, tn, transposed=False):
    if ws[0].dtype == BF16:
        return []
    return [pltpu.VMEM((tn, k) if transposed else (k, tn), BF16) for _ in ws]


def _mm_kernel(x_ref, xs_ref, w_ref, *refs, epilogue, n_heads, n_extra, n_out, transposed):
    extras = refs[:n_extra]
    outs = refs[n_extra:n_extra + n_out]
    souts = refs[n_extra + n_out:n_extra + 2 * n_out]
    first = pl.program_id(1) == 0
    (w,) = _weights([w_ref.at[0] if transposed else w_ref], refs[n_extra + 2 * n_out:], first)

    def project(x, outs):
        acc = lax.dot_general(x, w, NT, preferred_element_type=F32) if transposed else _dot(x, w)
        if epilogue == "sigmoid":
            acc = _sigmoid(acc)
        elif epilogue == "gdn_gate":
            a_ref, dt_ref = extras
            lane = _iota2(acc.shape, 1)
            g = -jnp.exp(a_ref[...]) * _softplus(acc + dt_ref[...])
            acc = jnp.where(lane < n_heads, _sigmoid(acc), g)
        for o in outs:
            o[...] = acc.astype(o.dtype)

    project(x_ref[...], outs)

    @pl.when(first)
    def _():
        project(xs_ref[...], souts)


def _mm(x, xs, w, n, *, layer, col0=0, transposed=False, out_dtypes=(F32,), epilogue=None, extras=(),
        n_heads=0, name="mm"):
    m, k = x.shape
    ms = xs.shape[0]
    tm = _tile(m, (1024, 512, 256))
    tn = _tile(n, (512, 256, 128))
    nd = len(out_dtypes)
    outs = pl.pallas_call(
        functools.partial(_mm_kernel, epilogue=epilogue, n_heads=n_heads, n_extra=len(extras), n_out=nd,
                          transposed=transposed),
        grid=(n // tn, m // tm),
        in_specs=[pl.BlockSpec((tm, k), lambda j, i: (i, 0)),
                  pl.BlockSpec((ms, k), lambda j, i: (0, 0)),
                  _w_spec(w, layer, k, tn, col0, transposed)]
                 + [pl.BlockSpec((1, tn), lambda j, i: (0, j)) for _ in extras],
        out_specs=[pl.BlockSpec((tm, tn), lambda j, i: (i, j)) for _ in out_dtypes]
                  + [pl.BlockSpec((ms, tn), lambda j, i: (0, j)) for _ in out_dtypes],
        out_shape=[jax.ShapeDtypeStruct((m, n), d) for d in out_dtypes]
                  + [jax.ShapeDtypeStruct((ms, n), d) for d in out_dtypes],
        scratch_shapes=_w_scratch([w], k, tn, transposed),
        compiler_params=_params("arbitrary", "arbitrary"),
        name=name,
    )(x, xs, w, *extras)
    if nd == 1:
        return outs[0], outs[1]
    return tuple(outs[:nd]), tuple(outs[nd:])


def _ffn_up_kernel(x_ref, xs_ref, wg_ref, wu_ref, o_ref, os_ref, *scratch):
    first = pl.program_id(1) == 0
    wg, wu = _weights([wg_ref, wu_ref], scratch, first)

    def swiglu(x, o):
        g = _dot(x, wg)
        u = _dot(x, wu)
        o[...] = (g * _sigmoid(g) * u).astype(o.dtype)

    swiglu(x_ref[...], o_ref)

    @pl.when(first)
    def _():
        swiglu(xs_ref[...], os_ref)


def _ffn_up(x, xs, wg, wu, layer):
    m, k = x.shape
    ms = xs.shape[0]
    n = wg.shape[-1]
    tm = _tile(m, (1024, 512, 256))
    tn = _tile(n, (256, 128))
    return pl.pallas_call(
        _ffn_up_kernel,
        grid=(n // tn, m // tm),
        in_specs=[pl.BlockSpec((tm, k), lambda j, i: (i, 0)),
                  pl.BlockSpec((ms, k), lambda j, i: (0, 0)),
                  _w_spec(wg, layer, k, tn), _w_spec(wu, layer, k, tn)],
        out_specs=[pl.BlockSpec((tm, tn), lambda j, i: (i, j)),
                   pl.BlockSpec((ms, tn), lambda j, i: (0, j))],
        out_shape=[jax.ShapeDtypeStruct((m, n), BF16), jax.ShapeDtypeStruct((ms, n), BF16)],
        scratch_shapes=_w_scratch([wg, wu], k, tn),
        compiler_params=_params("arbitrary", "arbitrary"),
        name="ffn_up",
    )(x, xs, wg, wu)


def _mm_res_kernel(h_ref, hs_ref, w_ref, x_ref, xs_ref, o_ref, os_ref, *scratch, alpha, scale):
    first = pl.program_id(1) == 0
    (w,) = _weights([w_ref], scratch, first)
    o_ref[...] = alpha * x_ref[...] + scale * _dot(h_ref[...], w)

    @pl.when(first)
    def _():
        os_ref[...] = alpha * xs_ref[...] + scale * _dot(hs_ref[...], w)


def _mm_res(h, hs, w, x, xs, alpha, scale, layer, name):
    m, k = h.shape
    ms = hs.shape[0]
    n = w.shape[-1]
    tm = _tile(m, (1024, 512, 256))
    tn = _tile(n, (256, 128))
    return pl.pallas_call(
        functools.partial(_mm_res_kernel, alpha=alpha, scale=scale),
        grid=(n // tn, m // tm),
        in_specs=[pl.BlockSpec((tm, k), lambda j, i: (i, 0)),
                  pl.BlockSpec((ms, k), lambda j, i: (0, 0)),
                  _w_spec(w, layer, k, tn),
                  pl.BlockSpec((tm, tn), lambda j, i: (i, j)),
                  pl.BlockSpec((ms, tn), lambda j, i: (0, j))],
        out_specs=[pl.BlockSpec((tm, tn), lambda j, i: (i, j)),
                   pl.BlockSpec((ms, tn), lambda j, i: (0, j))],
        out_shape=[jax.ShapeDtypeStruct((m, n), F32), jax.ShapeDtypeStruct((ms, n), F32)],
        scratch_shapes=_w_scratch([w], k, tn),
        compiler_params=_params("arbitrary", "arbitrary"),
        name=name,
    )(h, hs, w, x, xs)


def _mm_res_rows_kernel(h_ref, w_ref, x_ref, o_ref, *, alpha, scale):
    o_ref[...] = alpha * x_ref[...] + scale * _dot(h_ref[...], w_ref[...])


def _mm_res_rows(h, w, x, alpha, scale, name):
    m, k = h.shape
    n = w.shape[1]
    tm = _tile(m, (512, 256))
    tn = _tile(n, (256, 128))
    return pl.pallas_call(
        functools.partial(_mm_res_rows_kernel, alpha=alpha, scale=scale),
        grid=(m // tm, n // tn),
        in_specs=[pl.BlockSpec((tm, k), lambda i, j: (i, 0)),
                  pl.BlockSpec((k, tn), lambda i, j: (0, j)),
                  pl.BlockSpec((tm, tn), lambda i, j: (i, j))],
        out_specs=pl.BlockSpec((tm, tn), lambda i, j: (i, j)),
        out_shape=jax.ShapeDtypeStruct((m, n), F32),
        compiler_params=_params("parallel", "arbitrary"),
        name=name,
    )(h, w, x)


def _ln_kernel(r_ref, g_ref, b_ref, o_ref, ob_ref, *, rows):
    g = g_ref[...]
    b = b_ref[...]

    def body(i, carry):
        sl = pl.ds(pl.multiple_of(i * rows, rows), rows)
        r = r_ref[sl, :]
        mu = jnp.mean(r, axis=-1, keepdims=True)
        c = r - mu
        var = jnp.mean(c * c, axis=-1, keepdims=True)
        y = c * lax.rsqrt(var + LN_EPS) * g + b
        o_ref[sl, :] = y
        ob_ref[sl, :] = y.astype(BF16)
        return carry

    lax.fori_loop(0, r_ref.shape[0] // rows, body, 0)


def _ln(r, g, b):
    m, d = r.shape
    tr = _tile(m, (256,))
    rows = _tile(tr, (32, BF16_ROWS))
    return pl.pallas_call(
        functools.partial(_ln_kernel, rows=rows),
        grid=(m // tr,),
        in_specs=[pl.BlockSpec((tr, d), lambda i: (i, 0)),
                  pl.BlockSpec((1, d), lambda i: (0, 0)),
                  pl.BlockSpec((1, d), lambda i: (0, 0))],
        out_specs=[pl.BlockSpec((tr, d), lambda i: (i, 0)),
                   pl.BlockSpec((tr, d), lambda i: (i, 0))],
        out_shape=[jax.ShapeDtypeStruct((m, d), F32), jax.ShapeDtypeStruct((m, d), BF16)],
        compiler_params=_params("parallel"),
        name="layer_norm",
    )(r, g.reshape(1, d), b.reshape(1, d))


def _merge_kernel(*refs):
    prompt, sample, w_refs = refs[0:6], refs[6:12], refs[12:15]
    o_ref, os_ref = refs[15:17]
    first = pl.program_id(1) == 0
    wa, wb, wc = _weights(w_refs, refs[17:], first)

    def merge(oa_ref, ob_ref, oc_ref, ga_ref, gb_ref, gc_ref, o):
        y = ga_ref[...] * _dot(oa_ref[...], wa)
        y = y + gb_ref[...] * _dot(ob_ref[...], wb)
        y = y + gc_ref[...] * _dot(oc_ref[...], wc)
        o[...] = y.astype(o.dtype)

    merge(*prompt, o_ref)

    @pl.when(first)
    def _():
        merge(*sample, os_ref)


def _merge(branches, gates, branches_s, gates_s, wa, wb, wc, layer):
    m, k = branches[0].shape
    ms = branches_s[0].shape[0]
    d = wa.shape[-1]
    tm = _tile(m, (512, 256))
    tn = _tile(d, (256, 128))
    nb = d // tn
    o_spec = pl.BlockSpec((tm, k), lambda j, i: (i, 0))
    os_spec = pl.BlockSpec((ms, k), lambda j, i: (0, 0))
    w_spec = _w_spec(wa, layer, k, tn)
    return pl.pallas_call(
        _merge_kernel,
        grid=(nb, m // tm),
        in_specs=[o_spec, o_spec, o_spec]
                 + [pl.BlockSpec((tm, tn), lambda j, i, s=s: (i, s * nb + j)) for s in range(3)]
                 + [os_spec, os_spec, os_spec]
                 + [pl.BlockSpec((ms, tn), lambda j, i, s=s: (0, s * nb + j)) for s in range(3)]
                 + [w_spec, w_spec, w_spec],
        out_specs=[pl.BlockSpec((tm, tn), lambda j, i: (i, j)),
                   pl.BlockSpec((ms, tn), lambda j, i: (0, j))],
        out_shape=[jax.ShapeDtypeStruct((m, d), BF16), jax.ShapeDtypeStruct((ms, d), BF16)],
        scratch_shapes=_w_scratch([wa, wb, wc], k, tn),
        compiler_params=_params("arbitrary", "arbitrary"),
        name="branch_merge",
    )(*branches, gates, gates, gates, *branches_s, gates_s, gates_s, gates_s, wa, wb, wc)


def _causal_conv_silu(x, prev, cw):
    acc = x * cw[3:4, :]
    row = _iota2(prev.shape, 0)
    for s in (1, 2, 3):
        xs = pltpu.roll(x, s, 0)
        ps = pltpu.roll(prev, s, 0)
        head = jnp.where(row < s, ps, xs[:SUBLANES])
        xs = jnp.concatenate([head, xs[SUBLANES:]], axis=0)
        acc = acc + xs * cw[3 - s:4 - s, :]
    return acc * _sigmoid(acc)


def _l2norm(x):
    return x * lax.rsqrt(jnp.sum(x * x, axis=-1, keepdims=True) + NORM_EPS)


def _gdn_prep_kernel(xq_ref, xk_ref, xv_ref, pq_ref, pk_ref, pv_ref, cwq_ref, cwk_ref, cwv_ref,
                     beta_ref, g_ref, gt_ref,
                     u_ref, w_ref, qg_ref, kd_ref, qk_ref, eg_ref, *, hb, ng):
    has_prev = pl.program_id(1) > 0

    def conv(x_ref, p_ref, cw_ref):
        prev = jnp.where(has_prev, p_ref[0], 0.0)
        return _causal_conv_silu(x_ref[0], prev, cw_ref[...])

    qs = conv(xq_ref, pq_ref, cwq_ref)
    ks = conv(xk_ref, pk_ref, cwk_ref)
    vs = conv(xv_ref, pv_ref, cwv_ref)
    c = GDN_CHUNK
    n = hb * c
    groups = range(ng)

    def stack(x, gi):
        return jnp.concatenate([x[:, (gi * hb + h) * LANES:(gi * hb + h + 1) * LANES] for h in range(hb)], axis=0)

    def column(x, width=LANES):
        return jnp.concatenate(
            [jnp.broadcast_to(x[:, h:h + 1], (c, width)) for h in range(hb)], axis=0)

    r = _iota2((n, n), 0)
    cc = _iota2((n, n), 1)
    same = (r ^ cc) < c
    incl = same & (r >= cc)
    strict = same & (r > cc)
    eye = jnp.where(r == cc, 1.0, 0.0)
    r1 = _iota2((c, c), 0)
    c1 = _iota2((c, c), 1)
    tril = jnp.where(r1 >= c1, 1.0, 0.0).astype(BF16)
    triu = jnp.where(same & (r <= cc), 1.0, 0.0).astype(BF16)

    def setup(gi):
        q = _l2norm(stack(qs, gi)) * (LANES ** -0.5)
        k = _l2norm(stack(ks, gi))
        v = stack(vs, gi)
        g3 = _split3(g_ref[0, gi])
        gcw = column(_dot(tril, g3[0]) + _dot(tril, g3[1]) + _dot(tril, g3[2]), n)
        gcc = gcw[:, :LANES]
        t3 = _split3(gt_ref[0, gi, 0])
        gcr = (_dot(t3[0], triu) + _dot(t3[1], triu) + _dot(t3[2], triu))[0:1, :]
        beta = column(beta_ref[0, gi])
        decay = jnp.where(incl, jnp.exp(jnp.minimum(gcw - gcr, 0.0)), 0.0)
        kb = k * beta
        kbf = k.astype(BF16)
        kk = lax.dot_general(kb.astype(BF16), kbf, NT, preferred_element_type=F32)
        lmat = jnp.where(strict, kk * decay, 0.0)
        egc = jnp.exp(gcc)
        rhs = jnp.concatenate([v * beta, kb * egc], axis=1)
        qk = lax.dot_general(q.astype(BF16), kbf, NT, preferred_element_type=F32) * decay
        return dict(q=q, k=k, gcc=gcc, egc=egc, lmat=lmat, rhs=rhs, qk=qk)

    st = [setup(gi) for gi in groups]
    tinv = [eye - s["lmat"] for s in st]
    p = [_split(s["lmat"]) for s in st]
    for level in range(5):
        p = [_split(_dot_split(pg, pg)) for pg in p]
        tinv = [tg + _dot_split(_split(tg), pg) for tg, pg in zip(tinv, p)]
    sol = [_dot_split(_split(tg), _split(s["rhs"])) for tg, s in zip(tinv, st)]

    for gi in groups:
        s = st[gi]
        gcc = s["gcc"]
        glast = jnp.concatenate(
            [jnp.broadcast_to(gcc[(h + 1) * c - 1:(h + 1) * c, :], (c, LANES)) for h in range(hb)], axis=0)
        qg = (s["q"] * s["egc"]).astype(BF16)
        kd = (s["k"] * jnp.exp(glast - gcc)).astype(BF16)
        for h in range(hb):
            rows = slice(h * c, (h + 1) * c)
            hd = gi * hb + h
            u_ref[0, 0, hd] = sol[gi][rows, :LANES]
            w_ref[0, 0, hd] = sol[gi][rows, LANES:].astype(BF16)
            qg_ref[0, 0, hd] = qg[rows]
            kd_ref[0, 0, hd] = kd[rows]
            vcol = (h * c) // LANES
            blk = s["qk"][rows, vcol * LANES:(vcol + 1) * LANES]
            if (h * c) % LANES:
                blk = pltpu.roll(blk, LANES - (h * c) % LANES, 1)
            qk_ref[0, 0, hd] = blk[:, :c].astype(BF16)
            eg_ref[0, 0, hd] = jnp.exp(gcc[(h + 1) * c - 1:(h + 1) * c, :])


def _gdn_scan_kernel(u_ref, w_ref, qg_ref, kd_ref, qk_ref, eg_ref, z_ref, gn_ref, o_ref, s_ref, *, heads):
    @pl.when(pl.program_id(1) == 0)
    def _():
        s_ref[...] = jnp.zeros_like(s_ref)

    gn = gn_ref[...]
    for h in range(heads):
        s = s_ref[0, h]
        sb = s.astype(BF16)
        v_new = u_ref[0, 0, h] - _dot(w_ref[0, 0, h], sb)
        vb = v_new.astype(BF16)
        o = _dot(qg_ref[0, 0, h], sb) + _dot(qk_ref[0, 0, h], vb)
        s_ref[0, h] = s * eg_ref[0, 0, h] + lax.dot_general(kd_ref[0, 0, h], vb, TN,
                                                           preferred_element_type=F32)
        o = o * lax.rsqrt(jnp.mean(o * o, axis=-1, keepdims=True) + NORM_EPS) * gn
        z = z_ref[0, :, h * LANES:(h + 1) * LANES]
        o_ref[0, :, h * LANES:(h + 1) * LANES] = (o * (z * _sigmoid(z))).astype(o_ref.dtype)


def _gdn_prompt(zg, bg, conv_w, gdn_norm_g, heads):
    b, t, _ = zg.shape
    c = GDN_CHUNK
    n = t // c
    hb = GDN_HEADS_PER_STEP if heads % GDN_HEADS_PER_STEP == 0 else 1
    hg = heads // hb
    ng = GDN_GROUPS_PER_STEP if hg % GDN_GROUPS_PER_STEP == 0 else 1
    gw = heads * LANES
    wb = ng * hb * LANES
    nq = gw // wb

    def regroup(a):
        a = a.reshape(b, t, hg, hb).transpose(0, 2, 1, 3)
        return jnp.pad(a, ((0, 0), (0, 0), (0, 0), (0, LANES - hb)))

    beta = regroup(bg[:, :, :heads])
    g = bg[:, :, heads:2 * heads]
    gcol = regroup(g)
    gt = g.reshape(b, n, c, hg, hb).transpose(0, 3, 1, 4, 2).reshape(b, hg, n, 1, hb * c)
    gt = jnp.pad(gt, ((0, 0), (0, 0), (0, 0), (0, SUBLANES - 1), (0, 0)))

    def xspec(off):
        return pl.BlockSpec((1, c, wb), lambda i, j, h: (i, j, off + h))

    def pspec(off):
        return pl.BlockSpec((1, SUBLANES, wb),
                            lambda i, j, h: (i, jnp.maximum(j * (c // SUBLANES) - 1, 0), off + h))

    def cspec(off):
        return pl.BlockSpec((None, 4, wb), lambda i, j, h: (conv_w[1], 0, off + h))

    def ospec(rows, cols):
        return pl.BlockSpec((1, 1, ng * hb, rows, cols), lambda i, j, h: (i, j, h, 0, 0))

    def oshape(rows, cols, dt):
        return jax.ShapeDtypeStruct((b, n, heads, rows, cols), dt)

    cw = conv_w[0]
    u, w, qg, kd, qk, eg = pl.pallas_call(
        functools.partial(_gdn_prep_kernel, hb=hb, ng=ng),
        grid=(b, n, hg // ng),
        in_specs=[xspec(0), xspec(nq), xspec(2 * nq), pspec(0), pspec(nq), pspec(2 * nq),
                  cspec(0), cspec(nq), cspec(2 * nq),
                  pl.BlockSpec((1, ng, c, LANES), lambda i, j, h: (i, h, j, 0)),
                  pl.BlockSpec((1, ng, c, LANES), lambda i, j, h: (i, h, j, 0)),
                  pl.BlockSpec((1, ng, 1, SUBLANES, hb * c), lambda i, j, h: (i, h, j, 0, 0))],
        out_specs=[ospec(c, LANES), ospec(c, LANES), ospec(c, LANES), ospec(c, LANES),
                   ospec(c, c), ospec(1, LANES)],
        out_shape=[oshape(c, LANES, F32), oshape(c, LANES, BF16), oshape(c, LANES, BF16),
                   oshape(c, LANES, BF16), oshape(c, c, BF16), oshape(1, LANES, F32)],
        compiler_params=_params("parallel", "parallel", "parallel"),
        name="gdn_prep",
    )(zg, zg, zg, zg, zg, zg, cw, cw, cw, beta, gcol, gt)

    def sspec(rows, cols):
        return pl.BlockSpec((1, 1, heads, rows, cols), lambda i, j: (i, j, 0, 0, 0))

    o, s_fin = pl.pallas_call(
        functools.partial(_gdn_scan_kernel, heads=heads),
        grid=(b, n),
        in_specs=[sspec(c, LANES), sspec(c, LANES), sspec(c, LANES), sspec(c, LANES),
                  sspec(c, c), sspec(1, LANES),
                  pl.BlockSpec((1, c, gw), lambda i, j: (i, j, 3)),
                  pl.BlockSpec((1, LANES), lambda i, j: (0, 0))],
        out_specs=[pl.BlockSpec((1, c, gw), lambda i, j: (i, j, 0)),
                   pl.BlockSpec((1, heads, LANES, LANES), lambda i, j: (i, 0, 0, 0))],
        out_shape=[jax.ShapeDtypeStruct((b, t, gw), BF16),
                   jax.ShapeDtypeStruct((b, heads, LANES, LANES), F32)],
        compiler_params=_params("parallel", "arbitrary"),
        name="gdn_scan",
    )(u, w, qg, kd, qk, eg, zg, gdn_norm_g.reshape(1, LANES))
    return o, s_fin


def _gdn_sample_prep_kernel(x_ref, ctx_ref, cw_ref, o_ref, *, heads):
    cw = cw_ref[...]
    acc = x_ref[...] * cw[3:4, :]
    for j in range(3):
        acc = acc + ctx_ref[j] * cw[j:j + 1, :]
    act = acc * _sigmoid(acc)
    for i in range(3 * heads):
        sl = slice(i * LANES, (i + 1) * LANES)
        a = act[:, sl]
        if i < heads:
            a = _l2norm(a) * (LANES ** -0.5)
        elif i < 2 * heads:
            a = _l2norm(a)
        o_ref[:, sl] = a


def _gdn_step_kernel(s_ref, qc_ref, kc_ref, v_ref, z_ref, beta_ref, g_ref, gn_ref, so_ref, o_ref, *, heads):
    gn = gn_ref[...]
    for h in range(heads):
        s = s_ref[0, h] * jnp.exp(g_ref[0, h])
        kc = kc_ref[0, h]
        kv = jnp.sum(kc * s, axis=0, keepdims=True)
        delta = (v_ref[0, h] - kv) * beta_ref[0, h]
        s = s + kc * delta
        so_ref[0, h] = s
        o = jnp.sum(qc_ref[0, h] * s, axis=0, keepdims=True)
        o = o * lax.rsqrt(jnp.mean(o * o, axis=-1, keepdims=True) + NORM_EPS) * gn
        z = z_ref[0, h]
        o_ref[0, h] = o * (z * _sigmoid(z))


def _gdn_sample(zg, bg, state_conv, s0, conv_w, gdn_norm_g, heads):
    bsz = s0.shape[0]
    mp = zg.shape[0]
    gw = heads * LANES
    x = zg[:, :3 * gw]
    ctx = jnp.pad(state_conv.transpose(1, 0, 2), ((0, 0), (0, mp - bsz), (0, 0)))
    qkv = pl.pallas_call(
        functools.partial(_gdn_sample_prep_kernel, heads=heads),
        out_shape=jax.ShapeDtypeStruct((mp, 3 * gw), F32),
        compiler_params=pltpu.CompilerParams(vmem_limit_bytes=VMEM_LIMIT),
        name="gdn_sample_prep",
    )(x, ctx, conv_w[0][conv_w[1]])
    qkv = qkv[:bsz]

    def col(a):
        return a.reshape(bsz, heads, LANES, 1)

    def row(a):
        return a.reshape(bsz, heads, 1, LANES)

    def lanes(a):
        return jnp.broadcast_to(a[:, :, None, None], (bsz, heads, 1, LANES))

    cspec = pl.BlockSpec((1, heads, LANES, 1), lambda i: (i, 0, 0, 0))
    rspec = pl.BlockSpec((1, heads, 1, LANES), lambda i: (i, 0, 0, 0))
    sspec = pl.BlockSpec((1, heads, LANES, LANES), lambda i: (i, 0, 0, 0))
    s_new, o = pl.pallas_call(
        functools.partial(_gdn_step_kernel, heads=heads),
        grid=(bsz,),
        in_specs=[sspec, cspec, cspec, rspec, rspec, rspec, rspec,
                  pl.BlockSpec((1, LANES), lambda i: (0, 0))],
        out_specs=[sspec, rspec],
        out_shape=[jax.ShapeDtypeStruct(s0.shape, F32),
                   jax.ShapeDtypeStruct((bsz, heads, 1, LANES), F32)],
        compiler_params=_params("parallel"),
        name="gdn_step",
    )(s0, col(qkv[:, :gw]), col(qkv[:, gw:2 * gw]), row(qkv[:, 2 * gw:]), row(zg[:bsz, 3 * gw:]),
      lanes(bg[:bsz, :heads]), lanes(bg[:bsz, heads:2 * heads]), gdn_norm_g.reshape(1, LANES))
    conv_new = jnp.concatenate([state_conv[:, 1:], x[:bsz, None, :]], axis=1)
    return o.reshape(bsz, gw), s_new, conv_new


def _suffix_ones():
    r = _iota2((SB_BLOCK, 2 * SB_BLOCK), 0)
    c = _iota2((SB_BLOCK, 2 * SB_BLOCK), 1)
    return jnp.where((r > c) | (c >= SB_BLOCK), 1.0, 0.0).astype(BF16)


def _sb_prompt_kernel(bias_ref, q_ref, k_ref, v_ref, o_ref, later_ref, acc_ref, *, qrows):
    blk = SB_BLOCK
    nsub = qrows // blk
    first = pl.program_id(2) * nsub
    bias = bias_ref[pl.program_id(1)]
    ucat = _suffix_ones()
    later_ref[...] = jnp.zeros_like(later_ref)
    acc_ref[...] = jnp.zeros_like(acc_ref)

    def block(j, r0, dj):
        start = pl.multiple_of(j * blk, blk)
        kj = k_ref[0, pl.ds(start, blk), :]
        vj = v_ref[0, pl.ds(start, blk), :]
        z = lax.dot_general(q_ref[0, r0:, :], kj, NT, preferred_element_type=F32) * (LANES ** -0.5) + bias
        sp = jnp.maximum(z, 0.0) + jnp.log(1.0 + jnp.exp(-jnp.abs(z)))
        lk = -sp
        if dj is not None:
            causal = (_iota2(z.shape, 1) + dj * blk) < (_iota2(z.shape, 0) + r0)
            lk = jnp.where(causal, lk, 0.0)
        hi, lo = _split(lk)
        cs = _dot(hi, ucat) + _dot(lo, ucat)
        wgt = jnp.exp(z - sp + cs[:, :blk] + later_ref[r0:, :])
        if dj is not None:
            wgt = jnp.where(causal, wgt, 0.0)
        acc_ref[r0:, :] += _dot(wgt.astype(BF16), vj)
        later_ref[r0:, :] += cs[:, blk:]

    for dj in reversed(range(nsub)):
        block(first + dj, dj * blk, dj)

    def pair(t, carry):
        block(first - 1 - 2 * t, 0, None)
        block(first - 2 - 2 * t, 0, None)
        return carry

    if nsub % 2 == 0:
        lax.fori_loop(0, first // 2, pair, 0)
    else:
        lax.fori_loop(0, first, lambda t, cr: (block(first - 1 - t, 0, None), cr)[1], 0)
    o_ref[0] = acc_ref[...].astype(o_ref.dtype)


def _sb_prompt(qkv, bias, heads):
    b, t, _ = qkv.shape
    qrows = _tile(t, (SB_QROWS, 2 * SB_BLOCK, SB_BLOCK))
    return pl.pallas_call(
        functools.partial(_sb_prompt_kernel, qrows=qrows),
        grid_spec=pltpu.PrefetchScalarGridSpec(
            num_scalar_prefetch=1,
            grid=(b, heads, t // qrows),
            in_specs=[pl.BlockSpec((1, qrows, LANES), lambda bi, h, i, s: (bi, i, h)),
                      pl.BlockSpec((1, t, LANES), lambda bi, h, i, s: (bi, 0, heads + h)),
                      pl.BlockSpec((1, t, LANES), lambda bi, h, i, s: (bi, 0, 2 * heads + h))],
            out_specs=pl.BlockSpec((1, qrows, LANES), lambda bi, h, i, s: (bi, i, h)),
            scratch_shapes=[pltpu.VMEM((qrows, LANES), F32), pltpu.VMEM((qrows, LANES), F32)],
        ),
        out_shape=jax.ShapeDtypeStruct((b, t, heads * LANES), BF16),
        compiler_params=_params("parallel", "parallel", "arbitrary"),
        name="sb_prompt",
    )(bias, qkv, qkv, qkv)


def _sb_decode_kernel(pt_ref, q_ref, bias_ref, *refs, heads, pages):
    k_refs, v_refs = refs[:pages], refs[pages:2 * pages]
    o_ref, qbd_ref, acc_ref, later_ref = refs[2 * pages:]
    p = pl.program_id(1)
    page = SB_BLOCK

    @pl.when(p == 0)
    def _():
        acc_ref[...] = jnp.zeros_like(acc_ref)
        later_ref[...] = jnp.zeros_like(later_ref)
        qb = q_ref[0].astype(BF16)
        row = _iota2(qb.shape, 0)
        for h in range(heads):
            qbd_ref[:, h * LANES:(h + 1) * LANES] = jnp.where(row == h, qb, jnp.zeros_like(qb))

    def by_position(ref):
        return jnp.concatenate(
            [ref[0, 0, pl.ds(h, page, stride=heads), :] for h in range(heads)], axis=1).astype(BF16)

    ucat = _suffix_ones()
    qbd = qbd_ref[...]
    bias = bias_ref[...]
    zs = [lax.dot_general(qbd, by_position(k), NT, preferred_element_type=F32) * (LANES ** -0.5) + bias
          for k in k_refs]
    sps = [_softplus(z) for z in zs]
    splits = [_split(-sp) for sp in sps]
    css = [_dot(hi, ucat) + _dot(lo, ucat) for hi, lo in splits]
    later = later_ref[...]
    acc = acc_ref[...]
    for z, sp, cs, v in zip(zs, sps, css, v_refs):
        wgt = jnp.exp(z - sp + cs[:, :page] + later)
        acc = acc + _dot(wgt.astype(BF16), by_position(v))
        later = later + cs[:, page:]
    acc_ref[...] = acc
    later_ref[...] = later

    @pl.when(p == pl.num_programs(1) - 1)
    def _():
        for h in range(heads):
            o_ref[0, h:h + 1, :] = acc_ref[h:h + 1, h * LANES:(h + 1) * LANES]


def _sb_decode(q, cache_k, cache_v, layer, page_table, bias, heads):
    bsz, n_pages = page_table.shape
    rows = cache_k.shape[2]
    assert rows == SB_BLOCK * heads and heads & (heads - 1) == 0
    pages = SB_DECODE_PAGES if n_pages % SB_DECODE_PAGES == 0 else 1

    def kv_spec(s):
        return pl.BlockSpec((1, 1, rows, LANES),
                            lambda b, p, pt: (layer, pt[b, n_pages - 1 - (pages * p + s)], 0, 0))

    kv_specs = [kv_spec(s) for s in range(pages)]
    return pl.pallas_call(
        functools.partial(_sb_decode_kernel, heads=heads, pages=pages),
        grid_spec=pltpu.PrefetchScalarGridSpec(
            num_scalar_prefetch=1,
            grid=(bsz, n_pages // pages),
            in_specs=[pl.BlockSpec((1, heads, LANES), lambda b, p, pt: (b, 0, 0)),
                      pl.BlockSpec((heads, 1), lambda b, p, pt: (0, 0))] + kv_specs + kv_specs,
            out_specs=pl.BlockSpec((1, heads, LANES), lambda b, p, pt: (b, 0, 0)),
            scratch_shapes=[pltpu.VMEM((heads, heads * LANES), BF16),
                            pltpu.VMEM((heads, heads * LANES), F32),
                            pltpu.VMEM((heads, SB_BLOCK), F32)],
        ),
        out_shape=jax.ShapeDtypeStruct((bsz, heads, LANES), F32),
        compiler_params=_params("parallel", "arbitrary"),
        name="sb_decode",
    )(page_table, q, bias.reshape(heads, 1), *([cache_k] * pages), *([cache_v] * pages))


def _ln_rows(x, g, b):
    mu = jnp.mean(x, axis=-1, keepdims=True)
    c = x - mu
    var = jnp.mean(c * c, axis=-1, keepdims=True)
    return c * lax.rsqrt(var + LN_EPS) * g + b


def _mlp_prompt_kernel(u_ref, v_ref, g_ref, b_ref, ws_ref, bst_ref, o_ref, *, groups):
    vn = _ln_rows(_gelu(v_ref[...]), g_ref[...], b_ref[...]).astype(BF16)
    tri = _iota2((MLP_CHUNK, MLP_CHUNK), 0) >= _iota2((MLP_CHUNK, MLP_CHUNK), 1)
    bst = bst_ref[...]
    for gi in range(groups):
        sl = slice(gi * LANES, (gi + 1) * LANES)
        ws = jnp.where(tri, ws_ref[gi], 0.0).astype(BF16)
        mixed = _dot(ws, vn[:, sl]) + bst[:, gi:gi + 1]
        o_ref[:, sl] = (_gelu(u_ref[:, sl]) * mixed).astype(o_ref.dtype)


def _mlp_prompt(uv, ln_g, ln_b, ws, bs):
    m = uv.shape[0]
    groups = ws.shape[0]
    mw = groups * LANES
    return pl.pallas_call(
        functools.partial(_mlp_prompt_kernel, groups=groups),
        grid=(m // MLP_CHUNK,),
        in_specs=[pl.BlockSpec((MLP_CHUNK, mw), lambda i: (i, 0)),
                  pl.BlockSpec((MLP_CHUNK, mw), lambda i: (i, 1)),
                  pl.BlockSpec((1, mw), lambda i: (0, 0)),
                  pl.BlockSpec((1, mw), lambda i: (0, 0)),
                  pl.BlockSpec((groups, MLP_CHUNK, MLP_CHUNK), lambda i: (0, 0, 0)),
                  pl.BlockSpec((MLP_CHUNK, groups), lambda i: (0, 0))],
        out_specs=pl.BlockSpec((MLP_CHUNK, mw), lambda i: (i, 0)),
        out_shape=jax.ShapeDtypeStruct((m, mw), BF16),
        compiler_params=_params("parallel"),
        name="mlp_prompt",
    )(uv, uv, ln_g.reshape(1, mw), ln_b.reshape(1, mw), ws, bs.T)


def _mlp_sample_kernel(uv_ref, g_ref, b_ref, w_ref, c_ref, vn_ref, o_ref):
    mw = g_ref.shape[1]
    vn = _ln_rows(_gelu(uv_ref[:, mw:]), g_ref[...], b_ref[...])
    vn_ref[...] = vn
    o_ref[...] = _gelu(uv_ref[:, :mw]) * (vn * w_ref[...] + c_ref[...])


def _mlp_sample(uv, ln_g, ln_b, ws, bs):
    m = uv.shape[0]
    mw = ws.shape[0] * LANES
    wvec = jnp.repeat(ws[:, 0, 0], LANES).reshape(1, mw)
    cvec = jnp.repeat(bs[:, 0], LANES).reshape(1, mw)
    return pl.pallas_call(
        _mlp_sample_kernel,
        out_shape=[jax.ShapeDtypeStruct((m, mw), F32), jax.ShapeDtypeStruct((m, mw), F32)],
        compiler_params=pltpu.CompilerParams(vmem_limit_bytes=VMEM_LIMIT),
        name="mlp_sample",
    )(uv, ln_g.reshape(1, mw), ln_b.reshape(1, mw), wvec, cvec)


def kernel(x_prompt, x_sample, state_gdn, state_conv, cache_k, cache_v, page_table, ln1_g, ln1_b, ffn1_wg, ffn1_wu, ffn1_wd, w_in, conv_w, a_log, dt_bias, gdn_norm_g, sb_bias, mlp_ln_g, mlp_ln_b, mlp_ws, mlp_bs, w_branch_a, w_branch_b, w_branch_c, w_out, ln2_g, ln2_b, ffn2_wg, ffn2_wu, ffn2_wd, ln3_g, ln3_b):
    depth = w_in.shape[0]
    bp, t, d = x_prompt.shape
    bs_, ts, _ = x_sample.shape
    assert ts == 1, "the sample group decodes one token per sequence"
    gh = a_log.shape[1]
    gw = gh * LANES
    sh = sb_bias.shape[1]
    sw = sh * LANES
    mw = mlp_ln_g.shape[1]
    assert gdn_norm_g.shape[1] == LANES and mlp_ws.shape[2] == MLP_CHUNK
    assert t % MLP_CHUNK == 0 and t % SB_BLOCK == 0 and t % GDN_CHUNK == 0
    assert 2 * gh <= LANES
    alpha = float((2 * depth) ** 0.25)
    mp = bp * t
    ms = BF16_ROWS * pl.cdiv(bs_, BF16_ROWS)

    n_pool, page = cache_k.shape[1], cache_k.shape[2]
    assert page == SB_BLOCK
    assert (page_table.shape[1] * page) % MLP_CHUNK == 0, "the new token must open a chunk"
    ck = cache_k.reshape(depth, n_pool, page * sh, LANES)
    cv = cache_v.reshape(depth, n_pool, page * sh, LANES)

    xp = x_prompt.reshape(mp, d)
    xs = jnp.pad(x_sample.reshape(bs_, d), ((0, ms - bs_), (0, 0)))
    xpb = xp.astype(BF16)
    xsb = xs.astype(BF16)

    o_gate_gdn = 4 * gw
    o_sb = o_gate_gdn + 2 * gh
    pad_lane = lambda a: jnp.pad(a, (gh, LANES - 2 * gh)).reshape(1, LANES)

    w_in_t = jnp.swapaxes(w_in, 1, 2)
    o_mlp = o_sb + 3 * sw
    o_gates = o_mlp + 2 * mw
    pad_rows = lambda a: jnp.pad(a.astype(BF16), ((0, ms - bs_), (0, 0)))

    outs = [[] for _ in range(9)]
    for l in range(depth):
        wd1 = ffn1_wd[l].astype(BF16)
        wd2 = ffn2_wd[l].astype(BF16)
        a_vec, dt_vec = pad_lane(a_log[l]), pad_lane(dt_bias[l])

        def ffn(x, xb, xs, xsb, wg, wu, wd, g, b):
            h, hs = _ffn_up(xb, xsb, wg, wu, l)
            r = _mm_res_rows(h, wd, x, alpha, 0.5, "ffn_down")
            rs = _mm_res_rows(hs, wd, xs, alpha, 0.5, "ffn_down_sample")
            return _ln(r, g, b) + _ln(rs, g, b)

        def project(n, col0, name, **kw):
            return _mm(xpb, xsb, w_in_t, n, layer=l, col0=col0, transposed=True, name=name, **kw)

        xp, xpb, xs, xsb = ffn(xp, xpb, xs, xsb, ffn1_wg, ffn1_wu, wd1, ln1_g[l], ln1_b[l])
        zg, zg_s = project(4 * gw, 0, "in_gdn")
        bg, bg_s = project(LANES, o_gate_gdn, "in_gate_gdn", epilogue="gdn_gate", extras=(a_vec, dt_vec), n_heads=gh)
        (sb, sbb), (sb_s, _) = project(3 * sw, o_sb, "in_sb", out_dtypes=(F32, BF16))
        uv, uv_s = project(2 * mw, o_mlp, "in_mlp")
        gates, gates_s = project(3 * d, o_gates, "in_gates", epilogue="sigmoid")

        zg3 = zg.reshape(bp, t, 4 * gw)
        oa, s_fin = _gdn_prompt(zg3, bg.reshape(bp, t, LANES), (conv_w, l), gdn_norm_g[l], gh)
        ob = _sb_prompt(sbb.reshape(bp, t, 3 * sw), sb_bias[l], sh)
        oc = _mlp_prompt(uv, mlp_ln_g[l], mlp_ln_b[l], mlp_ws[l], mlp_bs[l])
        outs[0].append(s_fin)
        outs[1].append(zg3[:, t - 3:, :3 * gw])
        outs[2].append(sb[:, sw:2 * sw].reshape(bp, t, sh, LANES))
        outs[3].append(sb[:, 2 * sw:].reshape(bp, t, sh, LANES))

        oa_s, s_new, conv_new = _gdn_sample(zg_s, bg_s, state_conv[l], state_gdn[l], (conv_w, l), gdn_norm_g[l], gh)
        ob_s = _sb_decode(sb_s[:bs_, :sw].reshape(bs_, sh, LANES), ck, cv, l, page_table, sb_bias[l], sh)
        vn, oc_s = _mlp_sample(uv_s, mlp_ln_g[l], mlp_ln_b[l], mlp_ws[l], mlp_bs[l])

        merged, merged_s = _merge((oa.reshape(mp, gw), ob.reshape(mp, sw), oc), gates,
                                  (pad_rows(oa_s), pad_rows(ob_s.reshape(bs_, sw)), oc_s.astype(BF16)), gates_s,
                                  w_branch_a, w_branch_b, w_branch_c, l)
        r, rs = _mm_res(merged, merged_s, w_out, xp, xs, alpha, 1.0, l, "out_proj")
        xp, xpb, xs, xsb = _ln(r, ln2_g[l], ln2_b[l]) + _ln(rs, ln2_g[l], ln2_b[l])
        xp, xpb, xs, xsb = ffn(xp, xpb, xs, xsb, ffn2_wg, ffn2_wu, wd2, ln3_g[l], ln3_b[l])
        outs[4].append(s_new)
        outs[5].append(conv_new)
        outs[6].append(sb_s[:bs_, sw:2 * sw].reshape(bs_, 1, sh, LANES))
        outs[7].append(sb_s[:bs_, 2 * sw:].reshape(bs_, 1, sh, LANES))
        outs[8].append(vn[:bs_].reshape(bs_, 1, mw))

    st = [jnp.stack(o) for o in outs]
    return (xp.reshape(bp, t, d), xs[:bs_].reshape(bs_, 1, d), st[0], st[1], st[2], st[3],
            st[4], st[5], st[6], st[7], st[8])
```

```python
import functools

import jax
import jax.numpy as jnp
from jax import lax
from jax.experimental import pallas as pl
from jax.experimental.pallas import tpu as pltpu

F32 = jnp.float32
BF16 = jnp.bfloat16

LANES = 128
SUBLANES = 8
BF16_ROWS = 16
VMEM_LIMIT = 56 * 1024 * 1024
LN_EPS = 1e-5
NORM_EPS = 1e-6
GDN_CHUNK = 64
GDN_HEADS_PER_STEP = 4
GDN_GROUPS_PER_STEP = 4
SB_BLOCK = 128
SB_QROWS = 512
SB_DECODE_PAGES = 4
MLP_CHUNK = 128
WEIGHT_TRANSPOSE_PIECE = 512
NT =(((1,), (1,)), ((), ()))
TN = (((0,), (0,)), ((), ()))


def _params(*sem):
    return pltpu.CompilerParams(dimension_semantics=sem, vmem_limit_bytes=VMEM_LIMIT)


def _tile(n, prefs):
    for t in prefs:
        if n % t == 0:
            return t
    return n


def _dot(a, b):
    return jnp.dot(a, b, preferred_element_type=F32)


def _sigmoid(x):
    return 1.0 / (1.0 + jnp.exp(-x))


def _softplus(x):
    return jnp.maximum(x, 0.0) + jnp.log1p(jnp.exp(-jnp.abs(x)))


def _gelu(x):
    return 0.5 * x * (1.0 + jnp.tanh(0.7978845608028654 * (x + 0.044715 * (x * x * x))))


def _split(a):
    hi = a.astype(BF16)
    lo = (a - hi.astype(F32)).astype(BF16)
    return hi, lo


def _dot_split(a, b):
    return _dot(a[0], b[0]) + _dot(a[0], b[1]) + _dot(a[1], b[0])


def _split3(a):
    hi = a.astype(BF16)
    r = a - hi.astype(F32)
    mid = r.astype(BF16)
    lo = (r - mid.astype(F32)).astype(BF16)
    return hi, mid, lo


def _iota2(shape, dim):
    return lax.broadcasted_iota(jnp.int32, shape, dim)


def _weights(w_refs, scratch, first, transposed=False):
    if not scratch:
        return [w[...] for w in w_refs]

    @pl.when(first)
    def _():
        for w, s in zip(w_refs, scratch):
            if transposed:
                tn, k = w.shape
                step = _tile(k, (WEIGHT_TRANSPOSE_PIECE,))
                for c0 in range(0, k, step):
                    s[c0:c0 + step, :] = w[:, c0:c0 + step].T.astype(BF16)
            else:
                s[...] = w[...].astype(BF16)

    return [s[...] for s in scratch]


def _w_spec(w, layer, k, tn, col0=0, transposed=False):
    if transposed:
        assert col0 % SUBLANES == 0 and tn % SUBLANES == 0
        return pl.BlockSpec((pl.Element(1), pl.Element(tn), pl.Element(k)),
                            lambda j, i: (layer, pl.multiple_of(col0 + j * tn, SUBLANES), 0))
    assert col0 % tn == 0
    return pl.BlockSpec((None, k, tn), lambda j, i: (layer, 0, col0 // tn + j))


def _w_scratch(ws, k, tn, transposed=False):
    if ws[0].dtype == BF16:
        return []
    return [pltpu.VMEM((k, tn), BF16) for _ in ws]


def _mm_kernel(x_ref, xs_ref, w_ref, *refs, epilogue, n_heads, n_extra, n_out, transposed):
    extras = refs[:n_extra]
    outs = refs[n_extra:n_extra + n_out]
    souts = refs[n_extra + n_out:n_extra + 2 * n_out]
    first = pl.program_id(1) == 0
    (w,) = _weights([w_ref.at[0] if transposed else w_ref], refs[n_extra + 2 * n_out:], first, transposed)

    def project(x, outs):
        acc = _dot(x, w)
        if epilogue == "sigmoid":
            acc = _sigmoid(acc)
        elif epilogue == "gdn_gate":
            a_ref, dt_ref = extras
            lane = _iota2(acc.shape, 1)
            g = -jnp.exp(a_ref[...]) * _softplus(acc + dt_ref[...])
            acc = jnp.where(lane < n_heads, _sigmoid(acc), g)
        for o in outs:
            o[...] = acc.astype(o.dtype)

    project(x_ref[...], outs)

    @pl.when(first)
    def _():
        project(xs_ref[...], souts)


def _mm(x, xs, w, n, *, layer, col0=0, transposed=False, out_dtypes=(F32,), epilogue=None, extras=(),
        n_heads=0, name="mm"):
    m, k = x.shape
    ms = xs.shape[0]
    tm = _tile(m, (1024, 512, 256))
    tn = _tile(n, (512, 256, 128))
    nd = len(out_dtypes)
    outs = pl.pallas_call(
        functools.partial(_mm_kernel, epilogue=epilogue, n_heads=n_heads, n_extra=len(extras), n_out=nd,
                          transposed=transposed),
        grid=(n // tn, m // tm),
        in_specs=[pl.BlockSpec((tm, k), lambda j, i: (i, 0)),
                  pl.BlockSpec((ms, k), lambda j, i: (0, 0)),
                  _w_spec(w, layer, k, tn, col0, transposed)]
                 + [pl.BlockSpec((1, tn), lambda j, i: (0, j)) for _ in extras],
        out_specs=[pl.BlockSpec((tm, tn), lambda j, i: (i, j)) for _ in out_dtypes]
                  + [pl.BlockSpec((ms, tn), lambda j, i: (0, j)) for _ in out_dtypes],
        out_shape=[jax.ShapeDtypeStruct((m, n), d) for d in out_dtypes]
                  + [jax.ShapeDtypeStruct((ms, n), d) for d in out_dtypes],
        scratch_shapes=_w_scratch([w], k, tn, transposed),
        compiler_params=_params("arbitrary", "arbitrary"),
        name=name,
    )(x, xs, w, *extras)
    if nd == 1:
        return outs[0], outs[1]
    return tuple(outs[:nd]), tuple(outs[nd:])


def _ffn_up_kernel(x_ref, xs_ref, wg_ref, wu_ref, o_ref, os_ref, *scratch):
    first = pl.program_id(1) == 0
    wg, wu = _weights([wg_ref, wu_ref], scratch, first)

    def swiglu(x, o):
        g = _dot(x, wg)
        u = _dot(x, wu)
        o[...] = (g * _sigmoid(g) * u).astype(o.dtype)

    swiglu(x_ref[...], o_ref)

    @pl.when(first)
    def _():
        swiglu(xs_ref[...], os_ref)


def _ffn_up(x, xs, wg, wu, layer):
    m, k = x.shape
    ms = xs.shape[0]
    n = wg.shape[-1]
    tm = _tile(m, (1024, 512, 256))
    tn = _tile(n, (256, 128))
    return pl.pallas_call(
        _ffn_up_kernel,
        grid=(n // tn, m // tm),
        in_specs=[pl.BlockSpec((tm, k), lambda j, i: (i, 0)),
                  pl.BlockSpec((ms, k), lambda j, i: (0, 0)),
                  _w_spec(wg, layer, k, tn), _w_spec(wu, layer, k, tn)],
        out_specs=[pl.BlockSpec((tm, tn), lambda j, i: (i, j)),
                   pl.BlockSpec((ms, tn), lambda j, i: (0, j))],
        out_shape=[jax.ShapeDtypeStruct((m, n), BF16), jax.ShapeDtypeStruct((ms, n), BF16)],
        scratch_shapes=_w_scratch([wg, wu], k, tn),
        compiler_params=_params("arbitrary", "arbitrary"),
        name="ffn_up",
    )(x, xs, wg, wu)


def _mm_res_kernel(h_ref, hs_ref, w_ref, x_ref, xs_ref, o_ref, os_ref, *scratch, alpha, scale):
    first = pl.program_id(1) == 0
    (w,) = _weights([w_ref], scratch, first)
    o_ref[...] = alpha * x_ref[...] + scale * _dot(h_ref[...], w)

    @pl.when(first)
    def _():
        os_ref[...] = alpha * xs_ref[...] + scale * _dot(hs_ref[...], w)


def _mm_res(h, hs, w, x, xs, alpha, scale, layer, name):
    m, k = h.shape
    ms = hs.shape[0]
    n = w.shape[-1]
    tm = _tile(m, (512, 256))
    tn = _tile(n, (512, 256, 128))
    return pl.pallas_call(
        functools.partial(_mm_res_kernel, alpha=alpha, scale=scale),
        grid=(n // tn, m // tm),
        in_specs=[pl.BlockSpec((tm, k), lambda j, i: (i, 0)),
                  pl.BlockSpec((ms, k), lambda j, i: (0, 0)),
                  _w_spec(w, layer, k, tn),
                  pl.BlockSpec((tm, tn), lambda j, i: (i, j)),
                  pl.BlockSpec((ms, tn), lambda j, i: (0, j))],
        out_specs=[pl.BlockSpec((tm, tn), lambda j, i: (i, j)),
                   pl.BlockSpec((ms, tn), lambda j, i: (0, j))],
        out_shape=[jax.ShapeDtypeStruct((m, n), F32), jax.ShapeDtypeStruct((ms, n), F32)],
        scratch_shapes=_w_scratch([w], k, tn),
        compiler_params=_params("arbitrary", "arbitrary"),
        name=name,
    )(h, hs, w, x, xs)


def _mm_res_rows_kernel(h_ref, w_ref, x_ref, o_ref, *, alpha, scale):
    o_ref[...] = alpha * x_ref[...] + scale * _dot(h_ref[...], w_ref[...])


def _mm_res_rows(h, w, x, alpha, scale, name):
    m, k = h.shape
    n = w.shape[1]
    tm = _tile(m, (512, 256))
    tn = _tile(n, (256, 128))
    return pl.pallas_call(
        functools.partial(_mm_res_rows_kernel, alpha=alpha, scale=scale),
        grid=(m // tm, n // tn),
        in_specs=[pl.BlockSpec((tm, k), lambda i, j: (i, 0)),
                  pl.BlockSpec((k, tn), lambda i, j: (0, j)),
                  pl.BlockSpec((tm, tn), lambda i, j: (i, j))],
        out_specs=pl.BlockSpec((tm, tn), lambda i, j: (i, j)),
        out_shape=jax.ShapeDtypeStruct((m, n), F32),
        compiler_params=_params("parallel", "arbitrary"),
        name=name,
    )(h, w, x)


def _ln_kernel(r_ref, g_ref, b_ref, o_ref, ob_ref, *, rows):
    g = g_ref[...]
    b = b_ref[...]

    def body(i, carry):
        sl = pl.ds(pl.multiple_of(i * rows, rows), rows)
        r = r_ref[sl, :]
        mu = jnp.mean(r, axis=-1, keepdims=True)
        c = r - mu
        var = jnp.mean(c * c, axis=-1, keepdims=True)
        y = c * lax.rsqrt(var + LN_EPS) * g + b
        o_ref[sl, :] = y
        ob_ref[sl, :] = y.astype(BF16)
        return carry

    lax.fori_loop(0, r_ref.shape[0] // rows, body, 0)


def _ln(r, g, b):
    m, d = r.shape
    tr = _tile(m, (256,))
    rows = _tile(tr, (32, BF16_ROWS))
    return pl.pallas_call(
        functools.partial(_ln_kernel, rows=rows),
        grid=(m // tr,),
        in_specs=[pl.BlockSpec((tr, d), lambda i: (i, 0)),
                  pl.BlockSpec((1, d), lambda i: (0, 0)),
                  pl.BlockSpec((1, d), lambda i: (0, 0))],
        out_specs=[pl.BlockSpec((tr, d), lambda i: (i, 0)),
                   pl.BlockSpec((tr, d), lambda i: (i, 0))],
        out_shape=[jax.ShapeDtypeStruct((m, d), F32), jax.ShapeDtypeStruct((m, d), BF16)],
        compiler_params=_params("parallel"),
        name="layer_norm",
    )(r, g.reshape(1, d), b.reshape(1, d))


def _merge_kernel(*refs):
    prompt, sample, w_refs = refs[0:6], refs[6:12], refs[12:15]
    o_ref, os_ref = refs[15:17]
    first = pl.program_id(1) == 0
    wa, wb, wc = _weights(w_refs, refs[17:], first)

    def merge(oa_ref, ob_ref, oc_ref, ga_ref, gb_ref, gc_ref, o):
        y = ga_ref[...] * _dot(oa_ref[...], wa)
        y = y + gb_ref[...] * _dot(ob_ref[...], wb)
        y = y + gc_ref[...] * _dot(oc_ref[...], wc)
        o[...] = y.astype(o.dtype)

    merge(*prompt, o_ref)

    @pl.when(first)
    def _():
        merge(*sample, os_ref)


def _merge(branches, gates, branches_s, gates_s, wa, wb, wc, layer):
    m, k = branches[0].shape
    ms = branches_s[0].shape[0]
    d = wa.shape[-1]
    tm = _tile(m, (512, 256))
    tn = _tile(d, (256, 128))
    nb = d // tn
    o_spec = pl.BlockSpec((tm, k), lambda j, i: (i, 0))
    os_spec = pl.BlockSpec((ms, k), lambda j, i: (0, 0))
    w_spec = _w_spec(wa, layer, k, tn)
    return pl.pallas_call(
        _merge_kernel,
        grid=(nb, m // tm),
        in_specs=[o_spec, o_spec, o_spec]
                 + [pl.BlockSpec((tm, tn), lambda j, i, s=s: (i, s * nb + j)) for s in range(3)]
                 + [os_spec, os_spec, os_spec]
                 + [pl.BlockSpec((ms, tn), lambda j, i, s=s: (0, s * nb + j)) for s in range(3)]
                 + [w_spec, w_spec, w_spec],
        out_specs=[pl.BlockSpec((tm, tn), lambda j, i: (i, j)),
                   pl.BlockSpec((ms, tn), lambda j, i: (0, j))],
        out_shape=[jax.ShapeDtypeStruct((m, d), BF16), jax.ShapeDtypeStruct((ms, d), BF16)],
        scratch_shapes=_w_scratch([wa, wb, wc], k, tn),
        compiler_params=_params("arbitrary", "arbitrary"),
        name="branch_merge",
    )(*branches, gates, gates, gates, *branches_s, gates_s, gates_s, gates_s, wa, wb, wc)


def _causal_conv_silu(x, prev, cw):
    acc = x * cw[3:4, :]
    row = _iota2(prev.shape, 0)
    for s in (1, 2, 3):
        xs = pltpu.roll(x, s, 0)
        ps = pltpu.roll(prev, s, 0)
        head = jnp.where(row < s, ps, xs[:SUBLANES])
        xs = jnp.concatenate([head, xs[SUBLANES:]], axis=0)
        acc = acc + xs * cw[3 - s:4 - s, :]
    return acc * _sigmoid(acc)


def _l2norm(x):
    return x * lax.rsqrt(jnp.sum(x * x, axis=-1, keepdims=True) + NORM_EPS)


def _gdn_prep_kernel(xq_ref, xk_ref, xv_ref, pq_ref, pk_ref, pv_ref, cwq_ref, cwk_ref, cwv_ref,
                     beta_ref, g_ref, gt_ref,
                     u_ref, w_ref, qg_ref, kd_ref, qk_ref, eg_ref, *, hb, ng):
    has_prev = pl.program_id(1) > 0

    def conv(x_ref, p_ref, cw_ref):
        prev = jnp.where(has_prev, p_ref[0], 0.0)
        return _causal_conv_silu(x_ref[0], prev, cw_ref[...])

    qs = conv(xq_ref, pq_ref, cwq_ref)
    ks = conv(xk_ref, pk_ref, cwk_ref)
    vs = conv(xv_ref, pv_ref, cwv_ref)
    c = GDN_CHUNK
    n = hb * c
    groups = range(ng)

    def stack(x, gi):
        return jnp.concatenate([x[:, (gi * hb + h) * LANES:(gi * hb + h + 1) * LANES] for h in range(hb)], axis=0)

    def column(x, width=LANES):
        return jnp.concatenate(
            [jnp.broadcast_to(x[:, h:h + 1], (c, width)) for h in range(hb)], axis=0)

    r = _iota2((n, n), 0)
    cc = _iota2((n, n), 1)
    same = (r ^ cc) < c
    incl = same & (r >= cc)
    strict = same & (r > cc)
    eye = jnp.where(r == cc, 1.0, 0.0)
    r1 = _iota2((c, c), 0)
    c1 = _iota2((c, c), 1)
    tril = jnp.where(r1 >= c1, 1.0, 0.0).astype(BF16)
    triu = jnp.where(same & (r <= cc), 1.0, 0.0).astype(BF16)

    def setup(gi):
        q = _l2norm(stack(qs, gi)) * (LANES ** -0.5)
        k = _l2norm(stack(ks, gi))
        v = stack(vs, gi)
        g3 = _split3(g_ref[0, gi])
        gcw = column(_dot(tril, g3[0]) + _dot(tril, g3[1]) + _dot(tril, g3[2]), n)
        gcc = gcw[:, :LANES]
        t3 = _split3(gt_ref[0, gi, 0])
        gcr = (_dot(t3[0], triu) + _dot(t3[1], triu) + _dot(t3[2], triu))[0:1, :]
        beta = column(beta_ref[0, gi])
        decay = jnp.where(incl, jnp.exp(jnp.minimum(gcw - gcr, 0.0)), 0.0)
        kb = k * beta
        kbf = k.astype(BF16)
        kk = lax.dot_general(kb.astype(BF16), kbf, NT, preferred_element_type=F32)
        lmat = jnp.where(strict, kk * decay, 0.0)
        egc = jnp.exp(gcc)
        rhs = jnp.concatenate([v * beta, kb * egc], axis=1)
        qk = lax.dot_general(q.astype(BF16), kbf, NT, preferred_element_type=F32) * decay
        return dict(q=q, k=k, gcc=gcc, egc=egc, lmat=lmat, rhs=rhs, qk=qk)

    st = [setup(gi) for gi in groups]
    tinv = [eye - s["lmat"] for s in st]
    p = [_split(s["lmat"]) for s in st]
    for level in range(5):
        p = [_split(_dot_split(pg, pg)) for pg in p]
        tinv = [tg + _dot_split(_split(tg), pg) for tg, pg in zip(tinv, p)]
    sol = [_dot_split(_split(tg), _split(s["rhs"])) for tg, s in zip(tinv, st)]

    for gi in groups:
        s = st[gi]
        gcc = s["gcc"]
        glast = jnp.concatenate(
            [jnp.broadcast_to(gcc[(h + 1) * c - 1:(h + 1) * c, :], (c, LANES)) for h in range(hb)], axis=0)
        qg = (s["q"] * s["egc"]).astype(BF16)
        kd = (s["k"] * jnp.exp(glast - gcc)).astype(BF16)
        for h in range(hb):
            rows = slice(h * c, (h + 1) * c)
            hd = gi * hb + h
            u_ref[0, 0, hd] = sol[gi][rows, :LANES]
            w_ref[0, 0, hd] = sol[gi][rows, LANES:].astype(BF16)
            qg_ref[0, 0, hd] = qg[rows]
            kd_ref[0, 0, hd] = kd[rows]
            vcol = (h * c) // LANES
            blk = s["qk"][rows, vcol * LANES:(vcol + 1) * LANES]
            if (h * c) % LANES:
                blk = pltpu.roll(blk, LANES - (h * c) % LANES, 1)
            qk_ref[0, 0, hd] = blk[:, :c].astype(BF16)
            eg_ref[0, 0, hd] = jnp.exp(gcc[(h + 1) * c - 1:(h + 1) * c, :])


def _gdn_scan_kernel(u_ref, w_ref, qg_ref, kd_ref, qk_ref, eg_ref, z_ref, gn_ref, o_ref, s_ref, *, heads):
    @pl.when(pl.program_id(1) == 0)
    def _():
        s_ref[...] = jnp.zeros_like(s_ref)

    gn = gn_ref[...]
    for h in range(heads):
        s = s_ref[0, h]
        sb = s.astype(BF16)
        v_new = u_ref[0, 0, h] - _dot(w_ref[0, 0, h], sb)
        vb = v_new.astype(BF16)
        o = _dot(qg_ref[0, 0, h], sb) + _dot(qk_ref[0, 0, h], vb)
        s_ref[0, h] = s * eg_ref[0, 0, h] + lax.dot_general(kd_ref[0, 0, h], vb, TN,
                                                           preferred_element_type=F32)
        o = o * lax.rsqrt(jnp.mean(o * o, axis=-1, keepdims=True) + NORM_EPS) * gn
        z = z_ref[0, :, h * LANES:(h + 1) * LANES]
        o_ref[0, :, h * LANES:(h + 1) * LANES] = (o * (z * _sigmoid(z))).astype(o_ref.dtype)


def _gdn_prompt(zg, bg, conv_w, gdn_norm_g, heads):
    b, t, _ = zg.shape
    c = GDN_CHUNK
    n = t // c
    hb = GDN_HEADS_PER_STEP if heads % GDN_HEADS_PER_STEP == 0 else 1
    hg = heads // hb
    ng = GDN_GROUPS_PER_STEP if hg % GDN_GROUPS_PER_STEP == 0 else 1
    gw = heads * LANES
    wb = ng * hb * LANES
    nq = gw // wb

    def regroup(a):
        a = a.reshape(b, t, hg, hb).transpose(0, 2, 1, 3)
        return jnp.pad(a, ((0, 0), (0, 0), (0, 0), (0, LANES - hb)))

    beta = regroup(bg[:, :, :heads])
    g = bg[:, :, heads:2 * heads]
    gcol = regroup(g)
    gt = g.reshape(b, n, c, hg, hb).transpose(0, 3, 1, 4, 2).reshape(b, hg, n, 1, hb * c)
    gt = jnp.pad(gt, ((0, 0), (0, 0), (0, 0), (0, SUBLANES - 1), (0, 0)))

    def xspec(off):
        return pl.BlockSpec((1, c, wb), lambda i, j, h: (i, j, off + h))

    def pspec(off):
        return pl.BlockSpec((1, SUBLANES, wb),
                            lambda i, j, h: (i, jnp.maximum(j * (c // SUBLANES) - 1, 0), off + h))

    def cspec(off):
        return pl.BlockSpec((None, 4, wb), lambda i, j, h: (conv_w[1], 0, off + h))

    def ospec(rows, cols):
        return pl.BlockSpec((1, 1, ng * hb, rows, cols), lambda i, j, h: (i, j, h, 0, 0))

    def oshape(rows, cols, dt):
        return jax.ShapeDtypeStruct((b, n, heads, rows, cols), dt)

    cw = conv_w[0]
    u, w, qg, kd, qk, eg = pl.pallas_call(
        functools.partial(_gdn_prep_kernel, hb=hb, ng=ng),
        grid=(b, n, hg // ng),
        in_specs=[xspec(0), xspec(nq), xspec(2 * nq), pspec(0), pspec(nq), pspec(2 * nq),
                  cspec(0), cspec(nq), cspec(2 * nq),
                  pl.BlockSpec((1, ng, c, LANES), lambda i, j, h: (i, h, j, 0)),
                  pl.BlockSpec((1, ng, c, LANES), lambda i, j, h: (i, h, j, 0)),
                  pl.BlockSpec((1, ng, 1, SUBLANES, hb * c), lambda i, j, h: (i, h, j, 0, 0))],
        out_specs=[ospec(c, LANES), ospec(c, LANES), ospec(c, LANES), ospec(c, LANES),
                   ospec(c, c), ospec(1, LANES)],
        out_shape=[oshape(c, LANES, F32), oshape(c, LANES, BF16), oshape(c, LANES, BF16),
                   oshape(c, LANES, BF16), oshape(c, c, BF16), oshape(1, LANES, F32)],
        compiler_params=_params("parallel", "parallel", "parallel"),
        name="gdn_prep",
    )(zg, zg, zg, zg, zg, zg, cw, cw, cw, beta, gcol, gt)

    def sspec(rows, cols):
        return pl.BlockSpec((1, 1, heads, rows, cols), lambda i, j: (i, j, 0, 0, 0))

    o, s_fin = pl.pallas_call(
        functools.partial(_gdn_scan_kernel, heads=heads),
        grid=(b, n),
        in_specs=[sspec(c, LANES), sspec(c, LANES), sspec(c, LANES), sspec(c, LANES),
                  sspec(c, c), sspec(1, LANES),
                  pl.BlockSpec((1, c, gw), lambda i, j: (i, j, 3)),
                  pl.BlockSpec((1, LANES), lambda i, j: (0, 0))],
        out_specs=[pl.BlockSpec((1, c, gw), lambda i, j: (i, j, 0)),
                   pl.BlockSpec((1, heads, LANES, LANES), lambda i, j: (i, 0, 0, 0))],
        out_shape=[jax.ShapeDtypeStruct((b, t, gw), BF16),
                   jax.ShapeDtypeStruct((b, heads, LANES, LANES), F32)],
        compiler_params=_params("parallel", "arbitrary"),
        name="gdn_scan",
    )(u, w, qg, kd, qk, eg, zg, gdn_norm_g.reshape(1, LANES))
    return o, s_fin


def _gdn_sample_prep_kernel(x_ref, ctx_ref, cw_ref, o_ref, *, heads):
    cw = cw_ref[...]
    acc = x_ref[...] * cw[3:4, :]
    for j in range(3):
        acc = acc + ctx_ref[j] * cw[j:j + 1, :]
    act = acc * _sigmoid(acc)
    for i in range(3 * heads):
        sl = slice(i * LANES, (i + 1) * LANES)
        a = act[:, sl]
        if i < heads:
            a = _l2norm(a) * (LANES ** -0.5)
        elif i < 2 * heads:
            a = _l2norm(a)
        o_ref[:, sl] = a


def _gdn_step_kernel(s_ref, qc_ref, kc_ref, v_ref, z_ref, beta_ref, g_ref, gn_ref, so_ref, o_ref, *, heads):
    gn = gn_ref[...]
    for h in range(heads):
        s = s_ref[0, h] * jnp.exp(g_ref[0, h])
        kc = kc_ref[0, h]
        kv = jnp.sum(kc * s, axis=0, keepdims=True)
        delta = (v_ref[0, h] - kv) * beta_ref[0, h]
        s = s + kc * delta
        so_ref[0, h] = s
        o = jnp.sum(qc_ref[0, h] * s, axis=0, keepdims=True)
        o = o * lax.rsqrt(jnp.mean(o * o, axis=-1, keepdims=True) + NORM_EPS) * gn
        z = z_ref[0, h]
        o_ref[0, h] = o * (z * _sigmoid(z))


def _gdn_sample(zg, bg, state_conv, s0, conv_w, gdn_norm_g, heads):
    bsz = s0.shape[0]
    mp = zg.shape[0]
    gw = heads * LANES
    x = zg[:, :3 * gw]
    ctx = jnp.pad(state_conv.transpose(1, 0, 2), ((0, 0), (0, mp - bsz), (0, 0)))
    qkv = pl.pallas_call(
        functools.partial(_gdn_sample_prep_kernel, heads=heads),
        out_shape=jax.ShapeDtypeStruct((mp, 3 * gw), F32),
        compiler_params=pltpu.CompilerParams(vmem_limit_bytes=VMEM_LIMIT),
        name="gdn_sample_prep",
    )(x, ctx, conv_w[0][conv_w[1]])
    qkv = qkv[:bsz]

    def col(a):
        return a.reshape(bsz, heads, LANES, 1)

    def row(a):
        return a.reshape(bsz, heads, 1, LANES)

    def lanes(a):
        return jnp.broadcast_to(a[:, :, None, None], (bsz, heads, 1, LANES))

    cspec = pl.BlockSpec((1, heads, LANES, 1), lambda i: (i, 0, 0, 0))
    rspec = pl.BlockSpec((1, heads, 1, LANES), lambda i: (i, 0, 0, 0))
    sspec = pl.BlockSpec((1, heads, LANES, LANES), lambda i: (i, 0, 0, 0))
    s_new, o = pl.pallas_call(
        functools.partial(_gdn_step_kernel, heads=heads),
        grid=(bsz,),
        in_specs=[sspec, cspec, cspec, rspec, rspec, rspec, rspec,
                  pl.BlockSpec((1, LANES), lambda i: (0, 0))],
        out_specs=[sspec, rspec],
        out_shape=[jax.ShapeDtypeStruct(s0.shape, F32),
                   jax.ShapeDtypeStruct((bsz, heads, 1, LANES), F32)],
        compiler_params=_params("parallel"),
        name="gdn_step",
    )(s0, col(qkv[:, :gw]), col(qkv[:, gw:2 * gw]), row(qkv[:, 2 * gw:]), row(zg[:bsz, 3 * gw:]),
      lanes(bg[:bsz, :heads]), lanes(bg[:bsz, heads:2 * heads]), gdn_norm_g.reshape(1, LANES))
    conv_new = jnp.concatenate([state_conv[:, 1:], x[:bsz, None, :]], axis=1)
    return o.reshape(bsz, gw), s_new, conv_new


def _suffix_ones():
    r = _iota2((SB_BLOCK, 2 * SB_BLOCK), 0)
    c = _iota2((SB_BLOCK, 2 * SB_BLOCK), 1)
    return jnp.where((r > c) | (c >= SB_BLOCK), 1.0, 0.0).astype(BF16)


def _sb_prompt_kernel(bias_ref, q_ref, k_ref, v_ref, o_ref, later_ref, acc_ref, *, qrows):
    blk = SB_BLOCK
    nsub = qrows // blk
    first = pl.program_id(2) * nsub
    bias = bias_ref[pl.program_id(1)]
    ucat = _suffix_ones()
    later_ref[...] = jnp.zeros_like(later_ref)
    acc_ref[...] = jnp.zeros_like(acc_ref)

    def block(j, r0, dj):
        start = pl.multiple_of(j * blk, blk)
        kj = k_ref[0, pl.ds(start, blk), :]
        vj = v_ref[0, pl.ds(start, blk), :]
        z = lax.dot_general(q_ref[0, r0:, :], kj, NT, preferred_element_type=F32) * (LANES ** -0.5) + bias
        sp = jnp.maximum(z, 0.0) + jnp.log(1.0 + jnp.exp(-jnp.abs(z)))
        lk = -sp
        if dj is not None:
            causal = (_iota2(z.shape, 1) + dj * blk) < (_iota2(z.shape, 0) + r0)
            lk = jnp.where(causal, lk, 0.0)
        hi, lo = _split(lk)
        cs = _dot(hi, ucat) + _dot(lo, ucat)
        wgt = jnp.exp(z - sp + cs[:, :blk] + later_ref[r0:, :])
        if dj is not None:
            wgt = jnp.where(causal, wgt, 0.0)
        acc_ref[r0:, :] += _dot(wgt.astype(BF16), vj)
        later_ref[r0:, :] += cs[:, blk:]

    for dj in reversed(range(nsub)):
        block(first + dj, dj * blk, dj)

    def pair(t, carry):
        block(first - 1 - 2 * t, 0, None)
        block(first - 2 - 2 * t, 0, None)
        return carry

    if nsub % 2 == 0:
        lax.fori_loop(0, first // 2, pair, 0)
    else:
        lax.fori_loop(0, first, lambda t, cr: (block(first - 1 - t, 0, None), cr)[1], 0)
    o_ref[0] = acc_ref[...].astype(o_ref.dtype)


def _sb_prompt(qkv, bias, heads):
    b, t, _ = qkv.shape
    qrows = _tile(t, (SB_QROWS, 2 * SB_BLOCK, SB_BLOCK))
    return pl.pallas_call(
        functools.partial(_sb_prompt_kernel, qrows=qrows),
        grid_spec=pltpu.PrefetchScalarGridSpec(
            num_scalar_prefetch=1,
            grid=(b, heads, t // qrows),
            in_specs=[pl.BlockSpec((1, qrows, LANES), lambda bi, h, i, s: (bi, i, h)),
                      pl.BlockSpec((1, t, LANES), lambda bi, h, i, s: (bi, 0, heads + h)),
                      pl.BlockSpec((1, t, LANES), lambda bi, h, i, s: (bi, 0, 2 * heads + h))],
            out_specs=pl.BlockSpec((1, qrows, LANES), lambda bi, h, i, s: (bi, i, h)),
            scratch_shapes=[pltpu.VMEM((qrows, LANES), F32), pltpu.VMEM((qrows, LANES), F32)],
        ),
        out_shape=jax.ShapeDtypeStruct((b, t, heads * LANES), BF16),
        compiler_params=_params("parallel", "parallel", "arbitrary"),
        name="sb_prompt",
    )(bias, qkv, qkv, qkv)


def _sb_decode_kernel(pt_ref, q_ref, bias_ref, *refs, heads, pages):
    k_refs, v_refs = refs[:pages], refs[pages:2 * pages]
    o_ref, qbd_ref, acc_ref, later_ref = refs[2 * pages:]
    p = pl.program_id(1)
    page = SB_BLOCK

    @pl.when(p == 0)
    def _():
        acc_ref[...] = jnp.zeros_like(acc_ref)
        later_ref[...] = jnp.zeros_like(later_ref)
        qb = q_ref[0].astype(BF16)
        row = _iota2(qb.shape, 0)
        for h in range(heads):
            qbd_ref[:, h * LANES:(h + 1) * LANES] = jnp.where(row == h, qb, jnp.zeros_like(qb))

    def by_position(ref):
        return jnp.concatenate(
            [ref[0, 0, pl.ds(h, page, stride=heads), :] for h in range(heads)], axis=1).astype(BF16)

    ucat = _suffix_ones()
    qbd = qbd_ref[...]
    bias = bias_ref[...]
    zs = [lax.dot_general(qbd, by_position(k), NT, preferred_element_type=F32) * (LANES ** -0.5) + bias
          for k in k_refs]
    sps = [_softplus(z) for z in zs]
    splits = [_split(-sp) for sp in sps]
    css = [_dot(hi, ucat) + _dot(lo, ucat) for hi, lo in splits]
    later = later_ref[...]
    acc = acc_ref[...]
    for z, sp, cs, v in zip(zs, sps, css, v_refs):
        wgt = jnp.exp(z - sp + cs[:, :page] + later)
        acc = acc + _dot(wgt.astype(BF16), by_position(v))
        later = later + cs[:, page:]
    acc_ref[...] = acc
    later_ref[...] = later

    @pl.when(p == pl.num_programs(1) - 1)
    def _():
        for h in range(heads):
            o_ref[0, h:h + 1, :] = acc_ref[h:h + 1, h * LANES:(h + 1) * LANES]


def _sb_decode(q, cache_k, cache_v, layer, page_table, bias, heads):
    bsz, n_pages = page_table.shape
    rows = cache_k.shape[2]
    assert rows == SB_BLOCK * heads and heads & (heads - 1) == 0
    pages = SB_DECODE_PAGES if n_pages % SB_DECODE_PAGES == 0 else 1

    def kv_spec(s):
        return pl.BlockSpec((1, 1, rows, LANES),
                            lambda b, p, pt: (layer, pt[b, n_pages - 1 - (pages * p + s)], 0, 0))

    kv_specs = [kv_spec(s) for s in range(pages)]
    return pl.pallas_call(
        functools.partial(_sb_decode_kernel, heads=heads, pages=pages),
        grid_spec=pltpu.PrefetchScalarGridSpec(
            num_scalar_prefetch=1,
            grid=(bsz, n_pages // pages),
            in_specs=[pl.BlockSpec((1, heads, LANES), lambda b, p, pt: (b, 0, 0)),
                      pl.BlockSpec((heads, 1), lambda b, p, pt: (0, 0))] + kv_specs + kv_specs,
            out_specs=pl.BlockSpec((1, heads, LANES), lambda b, p, pt: (b, 0, 0)),
            scratch_shapes=[pltpu.VMEM((heads, heads * LANES), BF16),
                            pltpu.VMEM((heads, heads * LANES), F32),
                            pltpu.VMEM((heads, SB_BLOCK), F32)],
        ),
        out_shape=jax.ShapeDtypeStruct((bsz, heads, LANES), F32),
        compiler_params=_params("parallel", "arbitrary"),
        name="sb_decode",
    )(page_table, q, bias.reshape(heads, 1), *([cache_k] * pages), *([cache_v] * pages))


def _ln_rows(x, g, b):
    mu = jnp.mean(x, axis=-1, keepdims=True)
    c = x - mu
    var = jnp.mean(c * c, axis=-1, keepdims=True)
    return c * lax.rsqrt(var + LN_EPS) * g + b


def _mlp_prompt_kernel(u_ref, v_ref, g_ref, b_ref, ws_ref, bst_ref, o_ref, *, groups):
    vn = _ln_rows(_gelu(v_ref[...]), g_ref[...], b_ref[...]).astype(BF16)
    tri = _iota2((MLP_CHUNK, MLP_CHUNK), 0) >= _iota2((MLP_CHUNK, MLP_CHUNK), 1)
    bst = bst_ref[...]
    for gi in range(groups):
        sl = slice(gi * LANES, (gi + 1) * LANES)
        ws = jnp.where(tri, ws_ref[gi], 0.0).astype(BF16)
        mixed = _dot(ws, vn[:, sl]) + bst[:, gi:gi + 1]
        o_ref[:, sl] = (_gelu(u_ref[:, sl]) * mixed).astype(o_ref.dtype)


def _mlp_prompt(uv, ln_g, ln_b, ws, bs):
    m = uv.shape[0]
    groups = ws.shape[0]
    mw = groups * LANES
    return pl.pallas_call(
        functools.partial(_mlp_prompt_kernel, groups=groups),
        grid=(m // MLP_CHUNK,),
        in_specs=[pl.BlockSpec((MLP_CHUNK, mw), lambda i: (i, 0)),
                  pl.BlockSpec((MLP_CHUNK, mw), lambda i: (i, 1)),
                  pl.BlockSpec((1, mw), lambda i: (0, 0)),
                  pl.BlockSpec((1, mw), lambda i: (0, 0)),
                  pl.BlockSpec((groups, MLP_CHUNK, MLP_CHUNK), lambda i: (0, 0, 0)),
                  pl.BlockSpec((MLP_CHUNK, groups), lambda i: (0, 0))],
        out_specs=pl.BlockSpec((MLP_CHUNK, mw), lambda i: (i, 0)),
        out_shape=jax.ShapeDtypeStruct((m, mw), BF16),
        compiler_params=_params("parallel"),
        name="mlp_prompt",
    )(uv, uv, ln_g.reshape(1, mw), ln_b.reshape(1, mw), ws, bs.T)


def _mlp_sample_kernel(uv_ref, g_ref, b_ref, w_ref, c_ref, vn_ref, o_ref):
    mw = g_ref.shape[1]
    vn = _ln_rows(_gelu(uv_ref[:, mw:]), g_ref[...], b_ref[...])
    vn_ref[...] = vn
    o_ref[...] = _gelu(uv_ref[:, :mw]) * (vn * w_ref[...] + c_ref[...])


def _mlp_sample(uv, ln_g, ln_b, ws, bs):
    m = uv.shape[0]
    mw = ws.shape[0] * LANES
    wvec = jnp.repeat(ws[:, 0, 0], LANES).reshape(1, mw)
    cvec = jnp.repeat(bs[:, 0], LANES).reshape(1, mw)
    return pl.pallas_call(
        _mlp_sample_kernel,
        out_shape=[jax.ShapeDtypeStruct((m, mw), F32), jax.ShapeDtypeStruct((m, mw), F32)],
        compiler_params=pltpu.CompilerParams(vmem_limit_bytes=VMEM_LIMIT),
        name="mlp_sample",
    )(uv, ln_g.reshape(1, mw), ln_b.reshape(1, mw), wvec, cvec)


def kernel(x_prompt, x_sample, state_gdn, state_conv, cache_k, cache_v, page_table, ln1_g, ln1_b, ffn1_wg, ffn1_wu, ffn1_wd, w_in, conv_w, a_log, dt_bias, gdn_norm_g, sb_bias, mlp_ln_g, mlp_ln_b, mlp_ws, mlp_bs, w_branch_a, w_branch_b, w_branch_c, w_out, ln2_g, ln2_b, ffn2_wg, ffn2_wu, ffn2_wd, ln3_g, ln3_b):
    depth = w_in.shape[0]
    bp, t, d = x_prompt.shape
    bs_, ts, _ = x_sample.shape
    assert ts == 1, "the sample group decodes one token per sequence"
    gh = a_log.shape[1]
    gw = gh * LANES
    sh = sb_bias.shape[1]
    sw = sh * LANES
    mw = mlp_ln_g.shape[1]
    assert gdn_norm_g.shape[1] == LANES and mlp_ws.shape[2] == MLP_CHUNK
    assert t % MLP_CHUNK == 0 and t % SB_BLOCK == 0 and t % GDN_CHUNK == 0
    assert 2 * gh <= LANES
    alpha = float((2 * depth) ** 0.25)
    mp = bp * t
    ms = BF16_ROWS * pl.cdiv(bs_, BF16_ROWS)

    n_pool, page = cache_k.shape[1], cache_k.shape[2]
    assert page == SB_BLOCK
    assert (page_table.shape[1] * page) % MLP_CHUNK == 0, "the new token must open a chunk"
    ck = cache_k.reshape(depth, n_pool, page * sh, LANES)
    cv = cache_v.reshape(depth, n_pool, page * sh, LANES)

    xp = x_prompt.reshape(mp, d)
    xs = jnp.pad(x_sample.reshape(bs_, d), ((0, ms - bs_), (0, 0)))
    xpb = xp.astype(BF16)
    xsb = xs.astype(BF16)

    o_gate_gdn = 4 * gw
    o_sb = o_gate_gdn + 2 * gh
    pad_lane = lambda a: jnp.pad(a, (gh, LANES - 2 * gh)).reshape(1, LANES)

    w_in_t = jnp.swapaxes(w_in, 1, 2)
    o_mlp = o_sb + 3 * sw
    o_gates = o_mlp + 2 * mw
    pad_rows = lambda a: jnp.pad(a.astype(BF16), ((0, ms - bs_), (0, 0)))

    outs = [[] for _ in range(9)]
    for l in range(depth):
        wd1 = ffn1_wd[l].astype(BF16)
        wd2 = ffn2_wd[l].astype(BF16)
        a_vec, dt_vec = pad_lane(a_log[l]), pad_lane(dt_bias[l])

        def ffn(x, xb, xs, xsb, wg, wu, wd, g, b):
            h, hs = _ffn_up(xb, xsb, wg, wu, l)
            r = _mm_res_rows(h, wd, x, alpha, 0.5, "ffn_down")
            rs = _mm_res_rows(hs, wd, xs, alpha, 0.5, "ffn_down_sample")
            return _ln(r, g, b) + _ln(rs, g, b)

        def project(n, col0, name, **kw):
            return _mm(xpb, xsb, w_in_t, n, layer=l, col0=col0, transposed=True, name=name, **kw)

        xp, xpb, xs, xsb = ffn(xp, xpb, xs, xsb, ffn1_wg, ffn1_wu, wd1, ln1_g[l], ln1_b[l])
        zg, zg_s = project(4 * gw, 0, "in_gdn")
        bg, bg_s = project(LANES, o_gate_gdn, "in_gate_gdn", epilogue="gdn_gate", extras=(a_vec, dt_vec), n_heads=gh)
        (sb, sbb), (sb_s, _) = project(3 * sw, o_sb, "in_sb", out_dtypes=(F32, BF16))
        uv, uv_s = project(2 * mw, o_mlp, "in_mlp")
        gates, gates_s = project(3 * d, o_gates, "in_gates", epilogue="sigmoid")

        zg3 = zg.reshape(bp, t, 4 * gw)
        oa, s_fin = _gdn_prompt(zg3, bg.reshape(bp, t, LANES), (conv_w, l), gdn_norm_g[l], gh)
        ob = _sb_prompt(sbb.reshape(bp, t, 3 * sw), sb_bias[l], sh)
        oc = _mlp_prompt(uv, mlp_ln_g[l], mlp_ln_b[l], mlp_ws[l], mlp_bs[l])
        outs[0].append(s_fin)
        outs[1].append(zg3[:, t - 3:, :3 * gw])
        outs[2].append(sb[:, sw:2 * sw].reshape(bp, t, sh, LANES))
        outs[3].append(sb[:, 2 * sw:].reshape(bp, t, sh, LANES))

        oa_s, s_new, conv_new = _gdn_sample(zg_s, bg_s, state_conv[l], state_gdn[l], (conv_w, l), gdn_norm_g[l], gh)
        ob_s = _sb_decode(sb_s[:bs_, :sw].reshape(bs_, sh, LANES), ck, cv, l, page_table, sb_bias[l], sh)
        vn, oc_s = _mlp_sample(uv_s, mlp_ln_g[l], mlp_ln_b[l], mlp_ws[l], mlp_bs[l])

        merged, merged_s = _merge((oa.reshape(mp, gw), ob.reshape(mp, sw), oc), gates,
                                  (pad_rows(oa_s), pad_rows(ob_s.reshape(bs_, sw)), oc_s.astype(BF16)), gates_s,
                                  w_branch_a, w_branch_b, w_branch_c, l)
        r, rs = _mm_res(merged, merged_s, w_out, xp, xs, alpha, 1.0, l, "out_proj")
        xp, xpb, xs, xsb = _ln(r, ln2_g[l], ln2_b[l]) + _ln(rs, ln2_g[l], ln2_b[l])
        xp, xpb, xs, xsb = ffn(xp, xpb, xs, xsb, ffn2_wg, ffn2_wu, wd2, ln3_g[l], ln3_b[l])
        outs[4].append(s_new)
        outs[5].append(conv_new)
        outs[6].append(sb_s[:bs_, sw:2 * sw].reshape(bs_, 1, sh, LANES))
        outs[7].append(sb_s[:bs_, 2 * sw:].reshape(bs_, 1, sh, LANES))
        outs[8].append(vn[:bs_].reshape(bs_, 1, mw))

    st = [jnp.stack(o) for o in outs]
    return (xp.reshape(bp, t, d), xs[:bs_].reshape(bs_, 1, d), st[0], st[1], st[2], st[3],
            st[4], st[5], st[6], st[7], st[8])
```

```python
import functools

import jax
import jax.numpy as jnp
from jax import lax
from jax.experimental import pallas as pl
from jax.experimental.pallas import tpu as pltpu

F32 = jnp.float32
BF16 = jnp.bfloat16

LANES = 128
SUBLANES = 8
BF16_ROWS = 16
VMEM_LIMIT = 56 * 1024 * 1024
LN_EPS = 1e-5
NORM_EPS = 1e-6
GDN_CHUNK = 64
GDN_HEADS_PER_STEP = 4
GDN_GROUPS_PER_STEP = 4
SB_BLOCK = 128
SB_QROWS = 512
SB_HEADS_PER_STEP = 4
SB_DECODE_PAGES = 4
MLP_CHUNK = 128
WEIGHT_TRANSPOSE_PIECE = 512
NT =(((1,), (1,)), ((), ()))
TN = (((0,), (0,)), ((), ()))


def _params(*sem):
    return pltpu.CompilerParams(dimension_semantics=sem, vmem_limit_bytes=VMEM_LIMIT)


def _tile(n, prefs):
    for t in prefs:
        if n % t == 0:
            return t
    return n


def _dot(a, b):
    return jnp.dot(a, b, preferred_element_type=F32)


def _sigmoid(x):
    return 1.0 / (1.0 + jnp.exp(-x))


def _softplus(x):
    return jnp.maximum(x, 0.0) + jnp.log1p(jnp.exp(-jnp.abs(x)))


def _gelu(x):
    return 0.5 * x * (1.0 + jnp.tanh(0.7978845608028654 * (x + 0.044715 * (x * x * x))))


def _split(a):
    hi = a.astype(BF16)
    lo = (a - hi.astype(F32)).astype(BF16)
    return hi, lo


def _dot_split(a, b):
    return _dot(a[0], b[0]) + _dot(a[0], b[1]) + _dot(a[1], b[0])


def _split3(a):
    hi = a.astype(BF16)
    r = a - hi.astype(F32)
    mid = r.astype(BF16)
    lo = (r - mid.astype(F32)).astype(BF16)
    return hi, mid, lo


def _iota2(shape, dim):
    return lax.broadcasted_iota(jnp.int32, shape, dim)


def _weights(w_refs, scratch, first, transposed=False):
    if not scratch:
        return [w[...] for w in w_refs]

    @pl.when(first)
    def _():
        for w, s in zip(w_refs, scratch):
            if transposed:
                tn, k = w.shape
                step = _tile(k, (WEIGHT_TRANSPOSE_PIECE,))
                for c0 in range(0, k, step):
                    s[c0:c0 + step, :] = w[:, c0:c0 + step].T.astype(BF16)
            else:
                s[...] = w[...].astype(BF16)

    return [s[...] for s in scratch]


def _w_spec(w, layer, k, tn, col0=0, transposed=False):
    if transposed:
        assert col0 % SUBLANES == 0 and tn % SUBLANES == 0
        return pl.BlockSpec((pl.Element(1), pl.Element(tn), pl.Element(k)),
                            lambda j, i: (layer, pl.multiple_of(col0 + j * tn, SUBLANES), 0))
    assert col0 % tn == 0
    return pl.BlockSpec((None, k, tn), lambda j, i: (layer, 0, col0 // tn + j))


def _w_scratch(ws, k, tn, transposed=False):
    if ws[0].dtype == BF16:
        return []
    return [pltpu.VMEM((k, tn), BF16) for _ in ws]


def _mm_kernel(x_ref, xs_ref, w_ref, *refs, epilogue, n_heads, n_extra, n_out, transposed):
    extras = refs[:n_extra]
    outs = refs[n_extra:n_extra + n_out]
    souts = refs[n_extra + n_out:n_extra + 2 * n_out]
    first = pl.program_id(1) == 0
    (w,) = _weights([w_ref.at[0] if transposed else w_ref], refs[n_extra + 2 * n_out:], first, transposed)

    def project(x, outs):
        acc = _dot(x, w)
        if epilogue == "sigmoid":
            acc = _sigmoid(acc)
        elif epilogue == "gdn_gate":
            a_ref, dt_ref = extras
            lane = _iota2(acc.shape, 1)
            g = -jnp.exp(a_ref[...]) * _softplus(acc + dt_ref[...])
            acc = jnp.where(lane < n_heads, _sigmoid(acc), g)
        for o in outs:
            o[...] = acc.astype(o.dtype)

    project(x_ref[...], outs)

    @pl.when(first)
    def _():
        project(xs_ref[...], souts)


def _mm(x, xs, w, n, *, layer, col0=0, transposed=False, out_dtypes=(F32,), epilogue=None, extras=(),
        n_heads=0, name="mm"):
    m, k = x.shape
    ms = xs.shape[0]
    tm = _tile(m, (1024, 512, 256))
    tn = _tile(n, (512, 256, 128))
    nd = len(out_dtypes)
    outs = pl.pallas_call(
        functools.partial(_mm_kernel, epilogue=epilogue, n_heads=n_heads, n_extra=len(extras), n_out=nd,
                          transposed=transposed),
        grid=(n // tn, m // tm),
        in_specs=[pl.BlockSpec((tm, k), lambda j, i: (i, 0)),
                  pl.BlockSpec((ms, k), lambda j, i: (0, 0)),
                  _w_spec(w, layer, k, tn, col0, transposed)]
                 + [pl.BlockSpec((1, tn), lambda j, i: (0, j)) for _ in extras],
        out_specs=[pl.BlockSpec((tm, tn), lambda j, i: (i, j)) for _ in out_dtypes]
                  + [pl.BlockSpec((ms, tn), lambda j, i: (0, j)) for _ in out_dtypes],
        out_shape=[jax.ShapeDtypeStruct((m, n), d) for d in out_dtypes]
                  + [jax.ShapeDtypeStruct((ms, n), d) for d in out_dtypes],
        scratch_shapes=_w_scratch([w], k, tn, transposed),
        compiler_params=_params("arbitrary", "arbitrary"),
        name=name,
    )(x, xs, w, *extras)
    if nd == 1:
        return outs[0], outs[1]
    return tuple(outs[:nd]), tuple(outs[nd:])


def _ffn_up_kernel(x_ref, xs_ref, wg_ref, wu_ref, o_ref, os_ref, *scratch):
    first = pl.program_id(1) == 0
    wg, wu = _weights([wg_ref, wu_ref], scratch, first)

    def swiglu(x, o):
        g = _dot(x, wg)
        u = _dot(x, wu)
        o[...] = (g * _sigmoid(g) * u).astype(o.dtype)

    swiglu(x_ref[...], o_ref)

    @pl.when(first)
    def _():
        swiglu(xs_ref[...], os_ref)


def _ffn_up(x, xs, wg, wu, layer):
    m, k = x.shape
    ms = xs.shape[0]
    n = wg.shape[-1]
    tm = _tile(m, (1024, 512, 256))
    tn = _tile(n, (256, 128))
    return pl.pallas_call(
        _ffn_up_kernel,
        grid=(n // tn, m // tm),
        in_specs=[pl.BlockSpec((tm, k), lambda j, i: (i, 0)),
                  pl.BlockSpec((ms, k), lambda j, i: (0, 0)),
                  _w_spec(wg, layer, k, tn), _w_spec(wu, layer, k, tn)],
        out_specs=[pl.BlockSpec((tm, tn), lambda j, i: (i, j)),
                   pl.BlockSpec((ms, tn), lambda j, i: (0, j))],
        out_shape=[jax.ShapeDtypeStruct((m, n), BF16), jax.ShapeDtypeStruct((ms, n), BF16)],
        scratch_shapes=_w_scratch([wg, wu], k, tn),
        compiler_params=_params("arbitrary", "arbitrary"),
        name="ffn_up",
    )(x, xs, wg, wu)


def _mm_res_kernel(h_ref, hs_ref, w_ref, x_ref, xs_ref, o_ref, os_ref, *scratch, alpha, scale):
    first = pl.program_id(1) == 0
    (w,) = _weights([w_ref], scratch, first)
    o_ref[...] = alpha * x_ref[...] + scale * _dot(h_ref[...], w)

    @pl.when(first)
    def _():
        os_ref[...] = alpha * xs_ref[...] + scale * _dot(hs_ref[...], w)


def _mm_res(h, hs, w, x, xs, alpha, scale, layer, name):
    m, k = h.shape
    ms = hs.shape[0]
    n = w.shape[-1]
    tm = _tile(m, (512, 256))
    tn = _tile(n, (512, 256, 128))
    return pl.pallas_call(
        functools.partial(_mm_res_kernel, alpha=alpha, scale=scale),
        grid=(n // tn, m // tm),
        in_specs=[pl.BlockSpec((tm, k), lambda j, i: (i, 0)),
                  pl.BlockSpec((ms, k), lambda j, i: (0, 0)),
                  _w_spec(w, layer, k, tn),
                  pl.BlockSpec((tm, tn), lambda j, i: (i, j)),
                  pl.BlockSpec((ms, tn), lambda j, i: (0, j))],
        out_specs=[pl.BlockSpec((tm, tn), lambda j, i: (i, j)),
                   pl.BlockSpec((ms, tn), lambda j, i: (0, j))],
        out_shape=[jax.ShapeDtypeStruct((m, n), F32), jax.ShapeDtypeStruct((ms, n), F32)],
        scratch_shapes=_w_scratch([w], k, tn),
        compiler_params=_params("arbitrary", "arbitrary"),
        name=name,
    )(h, hs, w, x, xs)


def _mm_res_rows_kernel(h_ref, w_ref, x_ref, o_ref, *, alpha, scale):
    o_ref[...] = alpha * x_ref[...] + scale * _dot(h_ref[...], w_ref[...])


def _mm_res_rows(h, w, x, alpha, scale, name):
    m, k = h.shape
    n = w.shape[1]
    tm = _tile(m, (512, 256))
    tn = _tile(n, (256, 128))
    return pl.pallas_call(
        functools.partial(_mm_res_rows_kernel, alpha=alpha, scale=scale),
        grid=(m // tm, n // tn),
        in_specs=[pl.BlockSpec((tm, k), lambda i, j: (i, 0)),
                  pl.BlockSpec((k, tn), lambda i, j: (0, j)),
                  pl.BlockSpec((tm, tn), lambda i, j: (i, j))],
        out_specs=pl.BlockSpec((tm, tn), lambda i, j: (i, j)),
        out_shape=jax.ShapeDtypeStruct((m, n), F32),
        compiler_params=_params("parallel", "arbitrary"),
        name=name,
    )(h, w, x)


def _ln_kernel(r_ref, g_ref, b_ref, o_ref, ob_ref, *, rows):
    g = g_ref[...]
    b = b_ref[...]

    def body(i, carry):
        sl = pl.ds(pl.multiple_of(i * rows, rows), rows)
        r = r_ref[sl, :]
        mu = jnp.mean(r, axis=-1, keepdims=True)
        c = r - mu
        var = jnp.mean(c * c, axis=-1, keepdims=True)
        y = c * lax.rsqrt(var + LN_EPS) * g + b
        o_ref[sl, :] = y
        ob_ref[sl, :] = y.astype(BF16)
        return carry

    lax.fori_loop(0, r_ref.shape[0] // rows, body, 0)


def _ln(r, g, b):
    m, d = r.shape
    tr = _tile(m, (256,))
    rows = _tile(tr, (32, BF16_ROWS))
    return pl.pallas_call(
        functools.partial(_ln_kernel, rows=rows),
        grid=(m // tr,),
        in_specs=[pl.BlockSpec((tr, d), lambda i: (i, 0)),
                  pl.BlockSpec((1, d), lambda i: (0, 0)),
                  pl.BlockSpec((1, d), lambda i: (0, 0))],
        out_specs=[pl.BlockSpec((tr, d), lambda i: (i, 0)),
                   pl.BlockSpec((tr, d), lambda i: (i, 0))],
        out_shape=[jax.ShapeDtypeStruct((m, d), F32), jax.ShapeDtypeStruct((m, d), BF16)],
        compiler_params=_params("parallel"),
        name="layer_norm",
    )(r, g.reshape(1, d), b.reshape(1, d))


def _merge_kernel(*refs):
    prompt, sample, w_refs = refs[0:6], refs[6:12], refs[12:15]
    o_ref, os_ref = refs[15:17]
    first = pl.program_id(1) == 0
    wa, wb, wc = _weights(w_refs, refs[17:], first)

    def merge(oa_ref, ob_ref, oc_ref, ga_ref, gb_ref, gc_ref, o):
        y = ga_ref[...] * _dot(oa_ref[...], wa)
        y = y + gb_ref[...] * _dot(ob_ref[...], wb)
        y = y + gc_ref[...] * _dot(oc_ref[...], wc)
        o[...] = y.astype(o.dtype)

    merge(*prompt, o_ref)

    @pl.when(first)
    def _():
        merge(*sample, os_ref)


def _merge(branches, gates, branches_s, gates_s, wa, wb, wc, layer):
    m, k = branches[0].shape
    ms = branches_s[0].shape[0]
    d = wa.shape[-1]
    tm = _tile(m, (512, 256))
    tn = _tile(d, (256, 128))
    nb = d // tn
    o_spec = pl.BlockSpec((tm, k), lambda j, i: (i, 0))
    os_spec = pl.BlockSpec((ms, k), lambda j, i: (0, 0))
    w_spec = _w_spec(wa, layer, k, tn)
    return pl.pallas_call(
        _merge_kernel,
        grid=(nb, m // tm),
        in_specs=[o_spec, o_spec, o_spec]
                 + [pl.BlockSpec((tm, tn), lambda j, i, s=s: (i, s * nb + j)) for s in range(3)]
                 + [os_spec, os_spec, os_spec]
                 + [pl.BlockSpec((ms, tn), lambda j, i, s=s: (0, s * nb + j)) for s in range(3)]
                 + [w_spec, w_spec, w_spec],
        out_specs=[pl.BlockSpec((tm, tn), lambda j, i: (i, j)),
                   pl.BlockSpec((ms, tn), lambda j, i: (0, j))],
        out_shape=[jax.ShapeDtypeStruct((m, d), BF16), jax.ShapeDtypeStruct((ms, d), BF16)],
        scratch_shapes=_w_scratch([wa, wb, wc], k, tn),
        compiler_params=_params("arbitrary", "arbitrary"),
        name="branch_merge",
    )(*branches, gates, gates, gates, *branches_s, gates_s, gates_s, gates_s, wa, wb, wc)


def _causal_conv_silu(x, prev, cw):
    acc = x * cw[3:4, :]
    row = _iota2(prev.shape, 0)
    for s in (1, 2, 3):
        xs = pltpu.roll(x, s, 0)
        ps = pltpu.roll(prev, s, 0)
        head = jnp.where(row < s, ps, xs[:SUBLANES])
        xs = jnp.concatenate([head, xs[SUBLANES:]], axis=0)
        acc = acc + xs * cw[3 - s:4 - s, :]
    return acc * _sigmoid(acc)


def _l2norm(x):
    return x * lax.rsqrt(jnp.sum(x * x, axis=-1, keepdims=True) + NORM_EPS)


def _gdn_prep_kernel(xq_ref, xk_ref, xv_ref, pq_ref, pk_ref, pv_ref, cwq_ref, cwk_ref, cwv_ref,
                     beta_ref, g_ref, gt_ref,
                     u_ref, w_ref, qg_ref, kd_ref, qk_ref, eg_ref, *, hb, ng):
    has_prev = pl.program_id(1) > 0

    def conv(x_ref, p_ref, cw_ref):
        prev = jnp.where(has_prev, p_ref[0], 0.0)
        return _causal_conv_silu(x_ref[0], prev, cw_ref[...])

    qs = conv(xq_ref, pq_ref, cwq_ref)
    ks = conv(xk_ref, pk_ref, cwk_ref)
    vs = conv(xv_ref, pv_ref, cwv_ref)
    c = GDN_CHUNK
    n = hb * c
    groups = range(ng)

    def stack(x, gi):
        return jnp.concatenate([x[:, (gi * hb + h) * LANES:(gi * hb + h + 1) * LANES] for h in range(hb)], axis=0)

    def column(x, width=LANES):
        return jnp.concatenate(
            [jnp.broadcast_to(x[:, h:h + 1], (c, width)) for h in range(hb)], axis=0)

    r = _iota2((n, n), 0)
    cc = _iota2((n, n), 1)
    same = (r ^ cc) < c
    incl = same & (r >= cc)
    strict = same & (r > cc)
    eye = jnp.where(r == cc, 1.0, 0.0)
    r1 = _iota2((c, c), 0)
    c1 = _iota2((c, c), 1)
    tril = jnp.where(r1 >= c1, 1.0, 0.0).astype(BF16)
    triu = jnp.where(same & (r <= cc), 1.0, 0.0).astype(BF16)

    def setup(gi):
        q = _l2norm(stack(qs, gi)) * (LANES ** -0.5)
        k = _l2norm(stack(ks, gi))
        v = stack(vs, gi)
        g3 = _split3(g_ref[0, gi])
        gcw = column(_dot(tril, g3[0]) + _dot(tril, g3[1]) + _dot(tril, g3[2]), n)
        gcc = gcw[:, :LANES]
        t3 = _split3(gt_ref[0, gi, 0])
        gcr = (_dot(t3[0], triu) + _dot(t3[1], triu) + _dot(t3[2], triu))[0:1, :]
        beta = column(beta_ref[0, gi])
        decay = jnp.where(incl, jnp.exp(jnp.minimum(gcw - gcr, 0.0)), 0.0)
        kb = k * beta
        kbf = k.astype(BF16)
        kk = lax.dot_general(kb.astype(BF16), kbf, NT, preferred_element_type=F32)
        lmat = jnp.where(strict, kk * decay, 0.0)
        egc = jnp.exp(gcc)
        rhs = jnp.concatenate([v * beta, kb * egc], axis=1)
        qk = lax.dot_general(q.astype(BF16), kbf, NT, preferred_element_type=F32) * decay
        return dict(q=q, k=k, gcc=gcc, egc=egc, lmat=lmat, rhs=rhs, qk=qk)

    st = [setup(gi) for gi in groups]
    tinv = [eye - s["lmat"] for s in st]
    p = [_split(s["lmat"]) for s in st]
    for level in range(5):
        p = [_split(_dot_split(pg, pg)) for pg in p]
        tinv = [tg + _dot_split(_split(tg), pg) for tg, pg in zip(tinv, p)]
    sol = [_dot_split(_split(tg), _split(s["rhs"])) for tg, s in zip(tinv, st)]

    for gi in groups:
        s = st[gi]
        gcc = s["gcc"]
        glast = jnp.concatenate(
            [jnp.broadcast_to(gcc[(h + 1) * c - 1:(h + 1) * c, :], (c, LANES)) for h in range(hb)], axis=0)
        qg = (s["q"] * s["egc"]).astype(BF16)
        kd = (s["k"] * jnp.exp(glast - gcc)).astype(BF16)
        for h in range(hb):
            rows = slice(h * c, (h + 1) * c)
            hd = gi * hb + h
            u_ref[0, 0, hd] = sol[gi][rows, :LANES]
            w_ref[0, 0, hd] = sol[gi][rows, LANES:].astype(BF16)
            qg_ref[0, 0, hd] = qg[rows]
            kd_ref[0, 0, hd] = kd[rows]
            vcol = (h * c) // LANES
            blk = s["qk"][rows, vcol * LANES:(vcol + 1) * LANES]
            if (h * c) % LANES:
                blk = pltpu.roll(blk, LANES - (h * c) % LANES, 1)
            qk_ref[0, 0, hd] = blk[:, :c].astype(BF16)
            eg_ref[0, 0, hd] = jnp.exp(gcc[(h + 1) * c - 1:(h + 1) * c, :])


def _gdn_scan_kernel(u_ref, w_ref, qg_ref, kd_ref, qk_ref, eg_ref, z_ref, gn_ref, o_ref, s_ref, *, heads):
    @pl.when(pl.program_id(1) == 0)
    def _():
        s_ref[...] = jnp.zeros_like(s_ref)

    gn = gn_ref[...]
    for h in range(heads):
        s = s_ref[0, h]
        sb = s.astype(BF16)
        v_new = u_ref[0, 0, h] - _dot(w_ref[0, 0, h], sb)
        vb = v_new.astype(BF16)
        o = _dot(qg_ref[0, 0, h], sb) + _dot(qk_ref[0, 0, h], vb)
        s_ref[0, h] = s * eg_ref[0, 0, h] + lax.dot_general(kd_ref[0, 0, h], vb, TN,
                                                           preferred_element_type=F32)
        o = o * lax.rsqrt(jnp.mean(o * o, axis=-1, keepdims=True) + NORM_EPS) * gn
        z = z_ref[0, :, h * LANES:(h + 1) * LANES]
        o_ref[0, :, h * LANES:(h + 1) * LANES] = (o * (z * _sigmoid(z))).astype(o_ref.dtype)


def _gdn_prompt(zg, bg, conv_w, gdn_norm_g, heads):
    b, t, _ = zg.shape
    c = GDN_CHUNK
    n = t // c
    hb = GDN_HEADS_PER_STEP if heads % GDN_HEADS_PER_STEP == 0 else 1
    hg = heads // hb
    ng = GDN_GROUPS_PER_STEP if hg % GDN_GROUPS_PER_STEP == 0 else 1
    gw = heads * LANES
    wb = ng * hb * LANES
    nq = gw // wb

    def regroup(a):
        a = a.reshape(b, t, hg, hb).transpose(0, 2, 1, 3)
        return jnp.pad(a, ((0, 0), (0, 0), (0, 0), (0, LANES - hb)))

    beta = regroup(bg[:, :, :heads])
    g = bg[:, :, heads:2 * heads]
    gcol = regroup(g)
    gt = g.reshape(b, n, c, hg, hb).transpose(0, 3, 1, 4, 2).reshape(b, hg, n, 1, hb * c)
    gt = jnp.pad(gt, ((0, 0), (0, 0), (0, 0), (0, SUBLANES - 1), (0, 0)))

    def xspec(off):
        return pl.BlockSpec((1, c, wb), lambda i, j, h: (i, j, off + h))

    def pspec(off):
        return pl.BlockSpec((1, SUBLANES, wb),
                            lambda i, j, h: (i, jnp.maximum(j * (c // SUBLANES) - 1, 0), off + h))

    def cspec(off):
        return pl.BlockSpec((None, 4, wb), lambda i, j, h: (conv_w[1], 0, off + h))

    def ospec(rows, cols):
        return pl.BlockSpec((1, 1, ng * hb, rows, cols), lambda i, j, h: (i, j, h, 0, 0))

    def oshape(rows, cols, dt):
        return jax.ShapeDtypeStruct((b, n, heads, rows, cols), dt)

    cw = conv_w[0]
    u, w, qg, kd, qk, eg = pl.pallas_call(
        functools.partial(_gdn_prep_kernel, hb=hb, ng=ng),
        grid=(b, n, hg // ng),
        in_specs=[xspec(0), xspec(nq), xspec(2 * nq), pspec(0), pspec(nq), pspec(2 * nq),
                  cspec(0), cspec(nq), cspec(2 * nq),
                  pl.BlockSpec((1, ng, c, LANES), lambda i, j, h: (i, h, j, 0)),
                  pl.BlockSpec((1, ng, c, LANES), lambda i, j, h: (i, h, j, 0)),
                  pl.BlockSpec((1, ng, 1, SUBLANES, hb * c), lambda i, j, h: (i, h, j, 0, 0))],
        out_specs=[ospec(c, LANES), ospec(c, LANES), ospec(c, LANES), ospec(c, LANES),
                   ospec(c, c), ospec(1, LANES)],
        out_shape=[oshape(c, LANES, F32), oshape(c, LANES, BF16), oshape(c, LANES, BF16),
                   oshape(c, LANES, BF16), oshape(c, c, BF16), oshape(1, LANES, F32)],
        compiler_params=_params("parallel", "parallel", "parallel"),
        name="gdn_prep",
    )(zg, zg, zg, zg, zg, zg, cw, cw, cw, beta, gcol, gt)

    def sspec(rows, cols):
        return pl.BlockSpec((1, 1, heads, rows, cols), lambda i, j: (i, j, 0, 0, 0))

    o, s_fin = pl.pallas_call(
        functools.partial(_gdn_scan_kernel, heads=heads),
        grid=(b, n),
        in_specs=[sspec(c, LANES), sspec(c, LANES), sspec(c, LANES), sspec(c, LANES),
                  sspec(c, c), sspec(1, LANES),
                  pl.BlockSpec((1, c, gw), lambda i, j: (i, j, 3)),
                  pl.BlockSpec((1, LANES), lambda i, j: (0, 0))],
        out_specs=[pl.BlockSpec((1, c, gw), lambda i, j: (i, j, 0)),
                   pl.BlockSpec((1, heads, LANES, LANES), lambda i, j: (i, 0, 0, 0))],
        out_shape=[jax.ShapeDtypeStruct((b, t, gw), BF16),
                   jax.ShapeDtypeStruct((b, heads, LANES, LANES), F32)],
        compiler_params=_params("parallel", "arbitrary"),
        name="gdn_scan",
    )(u, w, qg, kd, qk, eg, zg, gdn_norm_g.reshape(1, LANES))
    return o, s_fin


def _gdn_sample_prep_kernel(x_ref, ctx_ref, cw_ref, o_ref, *, heads):
    cw = cw_ref[...]
    acc = x_ref[...] * cw[3:4, :]
    for j in range(3):
        acc = acc + ctx_ref[j] * cw[j:j + 1, :]
    act = acc * _sigmoid(acc)
    for i in range(3 * heads):
        sl = slice(i * LANES, (i + 1) * LANES)
        a = act[:, sl]
        if i < heads:
            a = _l2norm(a) * (LANES ** -0.5)
        elif i < 2 * heads:
            a = _l2norm(a)
        o_ref[:, sl] = a


def _gdn_step_kernel(s_ref, qc_ref, kc_ref, v_ref, z_ref, beta_ref, g_ref, gn_ref, so_ref, o_ref, *, heads):
    gn = gn_ref[...]
    for h in range(heads):
        s = s_ref[0, h] * jnp.exp(g_ref[0, h])
        kc = kc_ref[0, h]
        kv = jnp.sum(kc * s, axis=0, keepdims=True)
        delta = (v_ref[0, h] - kv) * beta_ref[0, h]
        s = s + kc * delta
        so_ref[0, h] = s
        o = jnp.sum(qc_ref[0, h] * s, axis=0, keepdims=True)
        o = o * lax.rsqrt(jnp.mean(o * o, axis=-1, keepdims=True) + NORM_EPS) * gn
        z = z_ref[0, h]
        o_ref[0, h] = o * (z * _sigmoid(z))


def _gdn_sample(zg, bg, state_conv, s0, conv_w, gdn_norm_g, heads):
    bsz = s0.shape[0]
    mp = zg.shape[0]
    gw = heads * LANES
    x = zg[:, :3 * gw]
    ctx = jnp.pad(state_conv.transpose(1, 0, 2), ((0, 0), (0, mp - bsz), (0, 0)))
    qkv = pl.pallas_call(
        functools.partial(_gdn_sample_prep_kernel, heads=heads),
        out_shape=jax.ShapeDtypeStruct((mp, 3 * gw), F32),
        compiler_params=pltpu.CompilerParams(vmem_limit_bytes=VMEM_LIMIT),
        name="gdn_sample_prep",
    )(x, ctx, conv_w[0][conv_w[1]])
    qkv = qkv[:bsz]

    def col(a):
        return a.reshape(bsz, heads, LANES, 1)

    def row(a):
        return a.reshape(bsz, heads, 1, LANES)

    def lanes(a):
        return jnp.broadcast_to(a[:, :, None, None], (bsz, heads, 1, LANES))

    cspec = pl.BlockSpec((1, heads, LANES, 1), lambda i: (i, 0, 0, 0))
    rspec = pl.BlockSpec((1, heads, 1, LANES), lambda i: (i, 0, 0, 0))
    sspec = pl.BlockSpec((1, heads, LANES, LANES), lambda i: (i, 0, 0, 0))
    s_new, o = pl.pallas_call(
        functools.partial(_gdn_step_kernel, heads=heads),
        grid=(bsz,),
        in_specs=[sspec, cspec, cspec, rspec, rspec, rspec, rspec,
                  pl.BlockSpec((1, LANES), lambda i: (0, 0))],
        out_specs=[sspec, rspec],
        out_shape=[jax.ShapeDtypeStruct(s0.shape, F32),
                   jax.ShapeDtypeStruct((bsz, heads, 1, LANES), F32)],
        compiler_params=_params("parallel"),
        name="gdn_step",
    )(s0, col(qkv[:, :gw]), col(qkv[:, gw:2 * gw]), row(qkv[:, 2 * gw:]), row(zg[:bsz, 3 * gw:]),
      lanes(bg[:bsz, :heads]), lanes(bg[:bsz, heads:2 * heads]), gdn_norm_g.reshape(1, LANES))
    conv_new = jnp.concatenate([state_conv[:, 1:], x[:bsz, None, :]], axis=1)
    return o.reshape(bsz, gw), s_new, conv_new


def _suffix_ones():
    r = _iota2((SB_BLOCK, 2 * SB_BLOCK), 0)
    c = _iota2((SB_BLOCK, 2 * SB_BLOCK), 1)
    return jnp.where((r > c) | (c >= SB_BLOCK), 1.0, 0.0).astype(BF16)


def _sb_prompt_kernel(bias_ref, q_ref, k_ref, v_ref, o_ref, later_ref, acc_ref, *, qrows, hp):
    blk = SB_BLOCK
    nsub = qrows // blk
    first = pl.program_id(2) * nsub
    biases = [bias_ref[pl.program_id(1) * hp + h] for h in range(hp)]
    lanes = [slice(h * LANES, (h + 1) * LANES) for h in range(hp)]
    ucat = _suffix_ones()
    later_ref[...] = jnp.zeros_like(later_ref)
    acc_ref[...] = jnp.zeros_like(acc_ref)

    def block(j, r0, dj):
        start = pl.multiple_of(j * blk, blk)
        zs = [lax.dot_general(q_ref[0, r0:, hl], k_ref[0, pl.ds(start, blk), hl], NT,
                              preferred_element_type=F32) * (LANES ** -0.5) + bias
              for hl, bias in zip(lanes, biases)]
        sps = [jnp.maximum(z, 0.0) + jnp.log(1.0 + jnp.exp(-jnp.abs(z))) for z in zs]
        lks = [-sp for sp in sps]
        if dj is not None:
            shape = zs[0].shape
            causal = (_iota2(shape, 1) + dj * blk) < (_iota2(shape, 0) + r0)
            lks = [jnp.where(causal, lk, 0.0) for lk in lks]
        splits = [_split(lk) for lk in lks]
        css = [_dot(hi, ucat) + _dot(lo, ucat) for hi, lo in splits]
        wgts = [jnp.exp(z - sp + cs[:, :blk] + later_ref[r0:, hl]) for z, sp, cs, hl in zip(zs, sps, css, lanes)]
        if dj is not None:
            wgts = [jnp.where(causal, wgt, 0.0) for wgt in wgts]
        for wgt, cs, hl in zip(wgts, css, lanes):
            acc_ref[r0:, hl] += _dot(wgt.astype(BF16), v_ref[0, pl.ds(start, blk), hl])
            later_ref[r0:, hl] += cs[:, blk:]

    for dj in reversed(range(nsub)):
        block(first + dj, dj * blk, dj)

    def pair(t, carry):
        block(first - 1 - 2 * t, 0, None)
        block(first - 2 - 2 * t, 0, None)
        return carry

    if nsub % 2 == 0:
        lax.fori_loop(0, first // 2, pair, 0)
    else:
        lax.fori_loop(0, first, lambda t, cr: (block(first - 1 - t, 0, None), cr)[1], 0)
    o_ref[0] = acc_ref[...].astype(o_ref.dtype)


def _sb_prompt(qkv, bias, heads):
    b, t, _ = qkv.shape
    qrows = _tile(t, (SB_QROWS, 2 * SB_BLOCK, SB_BLOCK))
    hp = SB_HEADS_PER_STEP if heads % SB_HEADS_PER_STEP == 0 else 1
    hg = heads // hp
    wl = hp * LANES
    return pl.pallas_call(
        functools.partial(_sb_prompt_kernel, qrows=qrows, hp=hp),
        grid_spec=pltpu.PrefetchScalarGridSpec(
            num_scalar_prefetch=1,
            grid=(b, hg, t // qrows),
            in_specs=[pl.BlockSpec((1, qrows, wl), lambda bi, h, i, s: (bi, i, h)),
                      pl.BlockSpec((1, t, wl), lambda bi, h, i, s: (bi, 0, hg + h)),
                      pl.BlockSpec((1, t, wl), lambda bi, h, i, s: (bi, 0, 2 * hg + h))],
            out_specs=pl.BlockSpec((1, qrows, wl), lambda bi, h, i, s: (bi, i, h)),
            scratch_shapes=[pltpu.VMEM((qrows, wl), F32), pltpu.VMEM((qrows, wl), F32)],
        ),
        out_shape=jax.ShapeDtypeStruct((b, t, heads * LANES), BF16),
        compiler_params=_params("parallel", "parallel", "arbitrary"),
        name="sb_prompt",
    )(bias, qkv, qkv, qkv)


def _sb_decode_kernel(pt_ref, q_ref, bias_ref, *refs, heads, pages):
    k_refs, v_refs = refs[:pages], refs[pages:2 * pages]
    o_ref, qbd_ref, acc_ref, later_ref = refs[2 * pages:]
    p = pl.program_id(1)
    page = SB_BLOCK

    @pl.when(p == 0)
    def _():
        acc_ref[...] = jnp.zeros_like(acc_ref)
        later_ref[...] = jnp.zeros_like(later_ref)
        qb = q_ref[0].astype(BF16)
        row = _iota2(qb.shape, 0)
        for h in range(heads):
            qbd_ref[:, h * LANES:(h + 1) * LANES] = jnp.where(row == h, qb, jnp.zeros_like(qb))

    def by_position(ref):
        return jnp.concatenate(
            [ref[0, 0, pl.ds(h, page, stride=heads), :] for h in range(heads)], axis=1).astype(BF16)

    ucat = _suffix_ones()
    qbd = qbd_ref[...]
    bias = bias_ref[...]
    zs = [lax.dot_general(qbd, by_position(k), NT, preferred_element_type=F32) * (LANES ** -0.5) + bias
          for k in k_refs]
    sps = [_softplus(z) for z in zs]
    splits = [_split(-sp) for sp in sps]
    css = [_dot(hi, ucat) + _dot(lo, ucat) for hi, lo in splits]
    later = later_ref[...]
    acc = acc_ref[...]
    for z, sp, cs, v in zip(zs, sps, css, v_refs):
        wgt = jnp.exp(z - sp + cs[:, :page] + later)
        acc = acc + _dot(wgt.astype(BF16), by_position(v))
        later = later + cs[:, page:]
    acc_ref[...] = acc
    later_ref[...] = later

    @pl.when(p == pl.num_programs(1) - 1)
    def _():
        for h in range(heads):
            o_ref[0, h:h + 1, :] = acc_ref[h:h + 1, h * LANES:(h + 1) * LANES]


def _sb_decode(q, cache_k, cache_v, layer, page_table, bias, heads):
    bsz, n_pages = page_table.shape
    rows = cache_k.shape[2]
    assert rows == SB_BLOCK * heads and heads & (heads - 1) == 0
    pages = SB_DECODE_PAGES if n_pages % SB_DECODE_PAGES == 0 else 1

    def kv_spec(s):
        return pl.BlockSpec((1, 1, rows, LANES),
                            lambda b, p, pt: (layer, pt[b, n_pages - 1 - (pages * p + s)], 0, 0))

    kv_specs = [kv_spec(s) for s in range(pages)]
    return pl.pallas_call(
        functools.partial(_sb_decode_kernel, heads=heads, pages=pages),
        grid_spec=pltpu.PrefetchScalarGridSpec(
            num_scalar_prefetch=1,
            grid=(bsz, n_pages // pages),
            in_specs=[pl.BlockSpec((1, heads, LANES), lambda b, p, pt: (b, 0, 0)),
                      pl.BlockSpec((heads, 1), lambda b, p, pt: (0, 0))] + kv_specs + kv_specs,
            out_specs=pl.BlockSpec((1, heads, LANES), lambda b, p, pt: (b, 0, 0)),
            scratch_shapes=[pltpu.VMEM((heads, heads * LANES), BF16),
                            pltpu.VMEM((heads, heads * LANES), F32),
                            pltpu.VMEM((heads, SB_BLOCK), F32)],
        ),
        out_shape=jax.ShapeDtypeStruct((bsz, heads, LANES), F32),
        compiler_params=_params("parallel", "arbitrary"),
        name="sb_decode",
    )(page_table, q, bias.reshape(heads, 1), *([cache_k] * pages), *([cache_v] * pages))


def _ln_rows(x, g, b):
    mu = jnp.mean(x, axis=-1, keepdims=True)
    c = x - mu
    var = jnp.mean(c * c, axis=-1, keepdims=True)
    return c * lax.rsqrt(var + LN_EPS) * g + b


def _mlp_prompt_kernel(u_ref, v_ref, g_ref, b_ref, ws_ref, bst_ref, o_ref, *, groups):
    vn = _ln_rows(_gelu(v_ref[...]), g_ref[...], b_ref[...]).astype(BF16)
    tri = _iota2((MLP_CHUNK, MLP_CHUNK), 0) >= _iota2((MLP_CHUNK, MLP_CHUNK), 1)
    bst = bst_ref[...]
    for gi in range(groups):
        sl = slice(gi * LANES, (gi + 1) * LANES)
        ws = jnp.where(tri, ws_ref[gi], 0.0).astype(BF16)
        mixed = _dot(ws, vn[:, sl]) + bst[:, gi:gi + 1]
        o_ref[:, sl] = (_gelu(u_ref[:, sl]) * mixed).astype(o_ref.dtype)


def _mlp_prompt(uv, ln_g, ln_b, ws, bs):
    m = uv.shape[0]
    groups = ws.shape[0]
    mw = groups * LANES
    return pl.pallas_call(
        functools.partial(_mlp_prompt_kernel, groups=groups),
        grid=(m // MLP_CHUNK,),
        in_specs=[pl.BlockSpec((MLP_CHUNK, mw), lambda i: (i, 0)),
                  pl.BlockSpec((MLP_CHUNK, mw), lambda i: (i, 1)),
                  pl.BlockSpec((1, mw), lambda i: (0, 0)),
                  pl.BlockSpec((1, mw), lambda i: (0, 0)),
                  pl.BlockSpec((groups, MLP_CHUNK, MLP_CHUNK), lambda i: (0, 0, 0)),
                  pl.BlockSpec((MLP_CHUNK, groups), lambda i: (0, 0))],
        out_specs=pl.BlockSpec((MLP_CHUNK, mw), lambda i: (i, 0)),
        out_shape=jax.ShapeDtypeStruct((m, mw), BF16),
        compiler_params=_params("parallel"),
        name="mlp_prompt",
    )(uv, uv, ln_g.reshape(1, mw), ln_b.reshape(1, mw), ws, bs.T)


def _mlp_sample_kernel(uv_ref, g_ref, b_ref, w_ref, c_ref, vn_ref, o_ref):
    mw = g_ref.shape[1]
    vn = _ln_rows(_gelu(uv_ref[:, mw:]), g_ref[...], b_ref[...])
    vn_ref[...] = vn
    o_ref[...] = _gelu(uv_ref[:, :mw]) * (vn * w_ref[...] + c_ref[...])


def _mlp_sample(uv, ln_g, ln_b, ws, bs):
    m = uv.shape[0]
    mw = ws.shape[0] * LANES
    wvec = jnp.repeat(ws[:, 0, 0], LANES).reshape(1, mw)
    cvec = jnp.repeat(bs[:, 0], LANES).reshape(1, mw)
    return pl.pallas_call(
        _mlp_sample_kernel,
        out_shape=[jax.ShapeDtypeStruct((m, mw), F32), jax.ShapeDtypeStruct((m, mw), F32)],
        compiler_params=pltpu.CompilerParams(vmem_limit_bytes=VMEM_LIMIT),
        name="mlp_sample",
    )(uv, ln_g.reshape(1, mw), ln_b.reshape(1, mw), wvec, cvec)


def kernel(x_prompt, x_sample, state_gdn, state_conv, cache_k, cache_v, page_table, ln1_g, ln1_b, ffn1_wg, ffn1_wu, ffn1_wd, w_in, conv_w, a_log, dt_bias, gdn_norm_g, sb_bias, mlp_ln_g, mlp_ln_b, mlp_ws, mlp_bs, w_branch_a, w_branch_b, w_branch_c, w_out, ln2_g, ln2_b, ffn2_wg, ffn2_wu, ffn2_wd, ln3_g, ln3_b):
    depth = w_in.shape[0]
    bp, t, d = x_prompt.shape
    bs_, ts, _ = x_sample.shape
    assert ts == 1, "the sample group decodes one token per sequence"
    gh = a_log.shape[1]
    gw = gh * LANES
    sh = sb_bias.shape[1]
    sw = sh * LANES
    mw = mlp_ln_g.shape[1]
    assert gdn_norm_g.shape[1] == LANES and mlp_ws.shape[2] == MLP_CHUNK
    assert t % MLP_CHUNK == 0 and t % SB_BLOCK == 0 and t % GDN_CHUNK == 0
    assert 2 * gh <= LANES
    alpha = float((2 * depth) ** 0.25)
    mp = bp * t
    ms = BF16_ROWS * pl.cdiv(bs_, BF16_ROWS)

    n_pool, page = cache_k.shape[1], cache_k.shape[2]
    assert page == SB_BLOCK
    assert (page_table.shape[1] * page) % MLP_CHUNK == 0, "the new token must open a chunk"
    ck = cache_k.reshape(depth, n_pool, page * sh, LANES)
    cv = cache_v.reshape(depth, n_pool, page * sh, LANES)

    xp = x_prompt.reshape(mp, d)
    xs = jnp.pad(x_sample.reshape(bs_, d), ((0, ms - bs_), (0, 0)))
    xpb = xp.astype(BF16)
    xsb = xs.astype(BF16)

    o_gate_gdn = 4 * gw
    o_sb = o_gate_gdn + 2 * gh
    pad_lane = lambda a: jnp.pad(a, (gh, LANES - 2 * gh)).reshape(1, LANES)

    w_in_t = jnp.swapaxes(w_in, 1, 2)
    o_mlp = o_sb + 3 * sw
    o_gates = o_mlp + 2 * mw
    pad_rows = lambda a: jnp.pad(a.astype(BF16), ((0, ms - bs_), (0, 0)))

    outs = [[] for _ in range(9)]
    for l in range(depth):
        wd1 = ffn1_wd[l].astype(BF16)
        wd2 = ffn2_wd[l].astype(BF16)
        a_vec, dt_vec = pad_lane(a_log[l]), pad_lane(dt_bias[l])

        def ffn(x, xb, xs, xsb, wg, wu, wd, g, b):
            h, hs = _ffn_up(xb, xsb, wg, wu, l)
            r = _mm_res_rows(h, wd, x, alpha, 0.5, "ffn_down")
            rs = _mm_res_rows(hs, wd, xs, alpha, 0.5, "ffn_down_sample")
            return _ln(r, g, b) + _ln(rs, g, b)

        def project(n, col0, name, **kw):
            return _mm(xpb, xsb, w_in_t, n, layer=l, col0=col0, transposed=True, name=name, **kw)

        xp, xpb, xs, xsb = ffn(xp, xpb, xs, xsb, ffn1_wg, ffn1_wu, wd1, ln1_g[l], ln1_b[l])
        zg, zg_s = project(4 * gw, 0, "in_gdn")
        bg, bg_s = project(LANES, o_gate_gdn, "in_gate_gdn", epilogue="gdn_gate", extras=(a_vec, dt_vec), n_heads=gh)
        (sb, sbb), (sb_s, _) = project(3 * sw, o_sb, "in_sb", out_dtypes=(F32, BF16))
        uv, uv_s = project(2 * mw, o_mlp, "in_mlp")
        gates, gates_s = project(3 * d, o_gates, "in_gates", epilogue="sigmoid")

        zg3 = zg.reshape(bp, t, 4 * gw)
        oa, s_fin = _gdn_prompt(zg3, bg.reshape(bp, t, LANES), (conv_w, l), gdn_norm_g[l], gh)
        ob = _sb_prompt(sbb.reshape(bp, t, 3 * sw), sb_bias[l], sh)
        oc = _mlp_prompt(uv, mlp_ln_g[l], mlp_ln_b[l], mlp_ws[l], mlp_bs[l])
        outs[0].append(s_fin)
        outs[1].append(zg3[:, t - 3:, :3 * gw])
        outs[2].append(sb[:, sw:2 * sw].reshape(bp, t, sh, LANES))
        outs[3].append(sb[:, 2 * sw:].reshape(bp, t, sh, LANES))

        oa_s, s_new, conv_new = _gdn_sample(zg_s, bg_s, state_conv[l], state_gdn[l], (conv_w, l), gdn_norm_g[l], gh)
        ob_s = _sb_decode(sb_s[:bs_, :sw].reshape(bs_, sh, LANES), ck, cv, l, page_table, sb_bias[l], sh)
        vn, oc_s = _mlp_sample(uv_s, mlp_ln_g[l], mlp_ln_b[l], mlp_ws[l], mlp_bs[l])

        merged, merged_s = _merge((oa.reshape(mp, gw), ob.reshape(mp, sw), oc), gates,
                                  (pad_rows(oa_s), pad_rows(ob_s.reshape(bs_, sw)), oc_s.astype(BF16)), gates_s,
                                  w_branch_a, w_branch_b, w_branch_c, l)
        r, rs = _mm_res(merged, merged_s, w_out, xp, xs, alpha, 1.0, l, "out_proj")
        xp, xpb, xs, xsb = _ln(r, ln2_g[l], ln2_b[l]) + _ln(rs, ln2_g[l], ln2_b[l])
        xp, xpb, xs, xsb = ffn(xp, xpb, xs, xsb, ffn2_wg, ffn2_wu, wd2, ln3_g[l], ln3_b[l])
        outs[4].append(s_new)
        outs[5].append(conv_new)
        outs[6].append(sb_s[:bs_, sw:2 * sw].reshape(bs_, 1, sh, LANES))
        outs[7].append(sb_s[:bs_, 2 * sw:].reshape(bs_, 1, sh, LANES))
        outs[8].append(vn[:bs_].reshape(bs_, 1, mw))

    st = [jnp.stack(o) for o in outs]
    return (xp.reshape(bp, t, d), xs[:bs_].reshape(bs_, 1, d), st[0], st[1], st[2], st[3],
            st[4], st[5], st[6], st[7], st[8])
```
